```python
import jax
import jax.numpy as jnp
from jax import lax
import numpy as np


D_MODEL = 4096
BATCH = 2
SEQ = 8192
DEPTH = 4

GRID_W = 64
CTX_LEN = 256
CHUNK = 64
EPS = 1e-6
F32 = jnp.float32

A_W = 3 * D_MODEL // 8
A_DK = 128
A_DV = 128
A_HEADS = A_W // A_DV
A_K = A_HEADS * A_DK
B_W = D_MODEL // 4
B_BW = 128
B_BLOCKS = B_W // B_BW
B_CONV = 4
RG_C = 8.0
C_W = D_MODEL - A_W - B_W
C_HEADS = 4
C_DV = C_W // C_HEADS
C_DK = C_DV // 2
C_K = C_HEADS * C_DK
C_RANK = 16
GLA_NORMALIZER = 16.0
IN_SIZES = (A_K, A_K, A_K, A_W, A_W, B_W, B_W, C_K, C_K, C_W, C_W, C_RANK, C_RANK)
IN_DIM = 3 * A_K + 2 * A_W + 2 * B_W + 2 * C_K + 2 * C_W + 2 * C_RANK
FFN_DIM = D_MODEL
FFN_CONV = 3

kernel_name = 'hybrid_hgrn2_rglru_gla_prefix_dit'


def rms_norm(x, g):
    xf = x.astype(F32)
    y = xf * lax.rsqrt(jnp.mean(xf * xf, axis=-1, keepdims=True) + EPS)
    return (y * g.astype(F32)).astype(x.dtype)


def head_rms(o, gain):
    Bn, L, H, V = o.shape
    y = o * lax.rsqrt(jnp.mean(o * o, axis=-1, keepdims=True) + EPS)
    return y.reshape(Bn, L, H * V) * gain.astype(F32)


def modulate(h, shift, scale):
    return h * (1.0 + scale) + shift


def split_cols(p, sizes):
    outs, off = [], 0
    for s in sizes:
        outs.append(p[..., off:off + s])
        off += s
    return outs


def flip(t):
    return jnp.flip(t, axis=1)


def chunk_gla(q, k, v, logf, s0):
    Bn, L, H, K = q.shape
    V = v.shape[-1]
    n = L // CHUNK

    def to_chunks(t):
        return t.astype(F32).reshape(Bn, n, CHUNK, H, t.shape[-1]).transpose(1, 0, 3, 2, 4)

    pos = jnp.arange(CHUNK)
    mask = (pos[:, None] >= pos[None, :])[:, :, None]
    maskf = mask.astype(F32)

    def step(S, inp):
        qc, kc, vc, gc = inp
        b = jnp.cumsum(gc, axis=2)
        diff = b[:, :, :, None, :] - b[:, :, None, :, :]
        dec = jnp.exp(jnp.where(mask, diff, 0.0)) * maskf
        att = jnp.einsum('bhtk,bhsk,bhtsk->bhts', qc, kc, dec)
        o = jnp.einsum('bhts,bhsv->bhtv', att, vc) + jnp.einsum('bhtk,bhkv->bhtv', qc * jnp.exp(b), S)
        b_last = b[:, :, -1:, :]
        S = jnp.exp(b_last[:, :, 0, :, None]) * S + jnp.einsum('bhsk,bhsv->bhkv', kc * jnp.exp(b_last - b), vc)
        return S, o

    S, o = lax.scan(step, s0.astype(F32), (to_chunks(q), to_chunks(k), to_chunks(v), to_chunks(logf)))
    o = o.transpose(1, 0, 3, 2, 4).reshape(Bn, L, H, V)
    return o, S


def two_way_gla(c_args, l_args):
    qc, kfc, kbc, vc, gfc, gbc = c_args
    ql, kfl, kbl, vl, gfl, gbl = l_args
    Bn, _, H, K = qc.shape
    V = vc.shape[-1]
    s0 = jnp.zeros((Bn, H, K, V), F32)
    oc_f, s_f = chunk_gla(qc, kfc, vc, gfc, s0)
    oc_b, s_b = chunk_gla(flip(qc), flip(kbc), flip(vc), flip(gbc), s0)
    ol_f, _ = chunk_gla(ql, kfl, vl, gfl, s_f)
    ol_b, _ = chunk_gla(flip(ql), flip(kbl), flip(vl), flip(gbl), s_b)
    return oc_f + flip(oc_b), ol_f + flip(ol_b)


def hgrn_forget(z, lb):
    f = lb + (1.0 - lb) * jax.nn.sigmoid(z)
    log_f = jnp.log(f)
    key = (1.0 - lb) * jax.nn.sigmoid(-z)
    return log_f, key


def hgrn2_group(pc, pl, lb_f, lb_b, gain, need_ctx):
    lbf = lb_f.reshape(A_HEADS, A_DK)
    lbb = lb_b.reshape(A_HEADS, A_DK)

    def prep(parts):
        q, zf, zb, i, _ = parts
        Bn, L, _ = q.shape
        heads = lambda t: t.astype(F32).reshape(Bn, L, A_HEADS, A_DK)
        qh = heads(jax.nn.silu(q)) * (A_DK ** -0.5)
        logf_f, k_f = hgrn_forget(heads(zf), lbf)
        logf_b, k_b = hgrn_forget(heads(zb), lbb)
        v = i.astype(F32).reshape(Bn, L, A_HEADS, A_DV)
        return (qh, k_f, k_b, v, logf_f, logf_b)

    oc, ol = two_way_gla(prep(pc), prep(pl))
    finish = lambda o, g: head_rms(o, gain) * jax.nn.silu(g.astype(F32))
    yl = finish(ol, pl[4])
    yc = finish(oc, pc[4]) if need_ctx else None
    return yc, yl


def gla_group(pc, pl, wa_f, ba_f, wa_b, ba_b, gain, need_ctx):
    def prep(parts):
        q, k, v, _, af, ab = parts
        Bn, L, _ = q.shape
        heads = lambda t: t.astype(F32).reshape(Bn, L, C_HEADS, C_DK)
        qh = heads(q) * (C_DK ** -0.5)
        kh = heads(k)
        vh = v.astype(F32).reshape(Bn, L, C_HEADS, C_DV)
        gf = heads(jax.nn.log_sigmoid(af.astype(F32) @ wa_f.astype(F32) + ba_f.astype(F32)) / GLA_NORMALIZER)
        gb = heads(jax.nn.log_sigmoid(ab.astype(F32) @ wa_b.astype(F32) + ba_b.astype(F32)) / GLA_NORMALIZER)
        return (qh, kh, kh, vh, gf, gb)

    oc, ol = two_way_gla(prep(pc), prep(pl))
    finish = lambda o, g: head_rms(o, gain) * jax.nn.silu(g.astype(F32))
    yl = finish(ol, pl[3])
    yc = finish(oc, pc[3]) if need_ctx else None
    return yc, yl


def dwconv1d(x, w, b):
    y = lax.conv_general_dilated(
        x, w[:, None, :], window_strides=(1,),
        padding=[(B_CONV // 2, B_CONV - 1 - B_CONV // 2)],
        dimension_numbers=('NWC', 'WIO', 'NWC'), feature_group_count=x.shape[-1])
    return y + b


def rglru_scan(x, w_r, b_r, w_i, b_i, lam, h0):
    Bn, L, W = x.shape
    xb = x.reshape(Bn, L, B_BLOCKS, B_BW)
    r = jax.nn.sigmoid(jnp.einsum('blni,nij->blnj', xb, w_r.astype(F32)).reshape(Bn, L, W) + b_r.astype(F32))
    i = jax.nn.sigmoid(jnp.einsum('blni,nij->blnj', xb, w_i.astype(F32)).reshape(Bn, L, W) + b_i.astype(F32))
    log_a = -RG_C * jax.nn.softplus(-lam.astype(F32)) * r
    a = jnp.exp(log_a)
    u = jnp.sqrt(jnp.maximum(-jnp.expm1(2.0 * log_a), 0.0)) * (i * x)
    u = u.at[:, 0].add(a[:, 0] * h0)

    def combine(left, right):
        a1, u1 = left
        a2, u2 = right
        return a1 * a2, a2 * u1 + u2

    _, h = lax.associative_scan(combine, (a, u), axis=1)
    return h


def rglru_group(pc, pl, conv_w, conv_b, fwd, bwd, gain, need_ctx):
    xc = dwconv1d(pc[0], conv_w, conv_b).astype(F32)
    xl = dwconv1d(pl[0], conv_w, conv_b).astype(F32)
    h0 = jnp.zeros((xc.shape[0], B_W), F32)
    hc_f = rglru_scan(xc, *fwd, h0)
    hc_b = flip(rglru_scan(flip(xc), *bwd, h0))
    hl = rglru_scan(xl, *fwd, hc_f[:, -1]) + flip(rglru_scan(flip(xl), *bwd, hc_b[:, 0]))
    finish = lambda h, gate: rms_norm(h, gain) * jax.nn.gelu(gate.astype(F32))
    yl = finish(hl, pl[1])
    yc = finish(hc_f + hc_b, pc[1]) if need_ctx else None
    return yc, yl


def conv_ffn(h, w_up, conv_w, conv_b, w_down, rows, cols):
    Bn, L, _ = h.shape
    up = h @ w_up
    u, gt = up[..., :FFN_DIM], up[..., FFN_DIM:]
    gt = gt.reshape(Bn, rows, cols, FFN_DIM)
    gt = lax.conv_general_dilated(
        gt, conv_w[:, :, None, :], window_strides=(1, 1), padding='SAME',
        dimension_numbers=('NHWC', 'HWIO', 'NHWC'), feature_group_count=FFN_DIM) + conv_b
    act = jax.nn.silu(gt.reshape(Bn, L, FFN_DIM)) * u
    return act @ w_down


def setup_inputs(seed: int = 0) -> dict:
    key = jax.random.key(seed)
    ks = iter(jax.random.split(key, 48))
    D = D_MODEL

    def nrm(shape, scale):
        return jax.random.normal(next(ks), shape, F32) * scale

    def gain(shape):
        return 1.0 + nrm(shape, 0.05)

    def lam_init():
        a0 = jax.random.uniform(next(ks), (DEPTH, B_W), F32, 0.9, 0.999)
        p = a0 ** (1.0 / RG_C)
        return jnp.log(p) - jnp.log1p(-p)

    return {
        'x': nrm((BATCH, SEQ, D), 1.0),
        'c': nrm((BATCH, D), 1.0),
        'ctx': nrm((BATCH, CTX_LEN, D), 1.0),
        'c_ctx': nrm((D,), 1.0),
        'w_ada': nrm((DEPTH, D, 6 * D), 0.2 * D ** -0.5),
        'b_ada': nrm((DEPTH, 6 * D), 0.02),
        'norm1_g': gain((DEPTH, D)),
        'w_in': nrm((DEPTH, D, IN_DIM), D ** -0.5),
        'hgrn_lb_fwd': nrm((DEPTH, A_K), 0.5),
        'hgrn_lb_bwd': nrm((DEPTH, A_K), 0.5),
        'hgrn_norm_g': gain((DEPTH, A_W)),
        'lru_conv_w': nrm((DEPTH, B_CONV, B_W), B_CONV ** -0.5),
        'lru_conv_b': nrm((DEPTH, B_W), 0.02),
        'lru_wr_fwd': nrm((DEPTH, B_BLOCKS, B_BW, B_BW), B_BW ** -0.5),
        'lru_br_fwd': nrm((DEPTH, B_W), 0.1),
        'lru_wi_fwd': nrm((DEPTH, B_BLOCKS, B_BW, B_BW), B_BW ** -0.5),
        'lru_bi_fwd': nrm((DEPTH, B_W), 0.1),
        'lru_lam_fwd': lam_init(),
        'lru_wr_bwd': nrm((DEPTH, B_BLOCKS, B_BW, B_BW), B_BW ** -0.5),
        'lru_br_bwd': nrm((DEPTH, B_W), 0.1),
        'lru_wi_bwd': nrm((DEPTH, B_BLOCKS, B_BW, B_BW), B_BW ** -0.5),
        'lru_bi_bwd': nrm((DEPTH, B_W), 0.1),
        'lru_lam_bwd': lam_init(),
        'lru_norm_g': gain((DEPTH, B_W)),
        'gla_wa_fwd': nrm((DEPTH, C_RANK, C_K), C_RANK ** -0.5),
        'gla_ba_fwd': nrm((DEPTH, C_K), 0.1),
        'gla_wa_bwd': nrm((DEPTH, C_RANK, C_K), C_RANK ** -0.5),
        'gla_ba_bwd': nrm((DEPTH, C_K), 0.1),
        'gla_norm_g': gain((DEPTH, C_W)),
        'w_out': nrm((DEPTH, A_W + B_W + C_W, D), (A_W + B_W + C_W) ** -0.5),
        'norm2_g': gain((DEPTH, D)),
        'w_up': nrm((DEPTH, D, 2 * FFN_DIM), D ** -0.5),
        'ffn_conv_w': nrm((DEPTH, FFN_CONV, FFN_CONV, FFN_DIM), 1.0 / FFN_CONV),
        'ffn_conv_b': nrm((DEPTH, FFN_DIM), 0.02),
        'w_down': nrm((DEPTH, FFN_DIM, D), FFN_DIM ** -0.5),
        'final_norm_g': gain((D,)),
    }


def reference(x, c, ctx, c_ctx, w_ada, b_ada, norm1_g, w_in, hgrn_lb_fwd, hgrn_lb_bwd, hgrn_norm_g,
              lru_conv_w, lru_conv_b, lru_wr_fwd, lru_br_fwd, lru_wi_fwd, lru_bi_fwd, lru_lam_fwd,
              lru_wr_bwd, lru_br_bwd, lru_wi_bwd, lru_bi_bwd, lru_lam_bwd, lru_norm_g,
              gla_wa_fwd, gla_ba_fwd, gla_wa_bwd, gla_ba_bwd, gla_norm_g, w_out, norm2_g,
              w_up, ffn_conv_w, ffn_conv_b, w_down, final_norm_g):
    rows = x.shape[1] // GRID_W

    def lower_bounds(logits):
        p = jax.nn.softmax(logits.astype(F32), axis=0)
        return jnp.clip(jnp.cumsum(p, axis=0) - p[0], 0.0, 1.0)

    lbf_all = lower_bounds(hgrn_lb_fwd)
    lbb_all = lower_bounds(hgrn_lb_bwd)
    silu_c = jax.nn.silu(c)
    silu_cc = jax.nn.silu(c_ctx)
    xl, xc = x, ctx
    for l in range(DEPTH):
        need_ctx = l < DEPTH - 1
        mod_l = (silu_c @ w_ada[l] + b_ada[l])[:, None, :]
        mod_c = silu_cc @ w_ada[l] + b_ada[l]
        sh1, sc1, g1, sh2, sc2, g2 = jnp.split(mod_l, 6, axis=-1)
        csh1, csc1, cg1, csh2, csc2, cg2 = jnp.split(mod_c, 6, axis=-1)

        hl = modulate(rms_norm(xl, norm1_g[l]), sh1, sc1)
        hc = modulate(rms_norm(xc, norm1_g[l]), csh1, csc1)
        pl = split_cols(hl @ w_in[l], IN_SIZES)
        pc = split_cols(hc @ w_in[l], IN_SIZES)
        ya_c, ya_l = hgrn2_group(pc[0:5], pl[0:5], lbf_all[l], lbb_all[l], hgrn_norm_g[l], need_ctx)
        fwd = (lru_wr_fwd[l], lru_br_fwd[l], lru_wi_fwd[l], lru_bi_fwd[l], lru_lam_fwd[l])
        bwd = (lru_wr_bwd[l], lru_br_bwd[l], lru_wi_bwd[l], lru_bi_bwd[l], lru_lam_bwd[l])
        yb_c, yb_l = rglru_group(pc[5:7], pl[5:7], lru_conv_w[l], lru_conv_b[l], fwd, bwd, lru_norm_g[l], need_ctx)
        yc_c, yc_l = gla_group(pc[7:13], pl[7:13], gla_wa_fwd[l], gla_ba_fwd[l], gla_wa_bwd[l], gla_ba_bwd[l],
                               gla_norm_g[l], need_ctx)
        y_l = jnp.concatenate([ya_l, yb_l, yc_l], axis=-1).astype(x.dtype)
        xl = xl + g1 * (y_l @ w_out[l])
        hl2 = modulate(rms_norm(xl, norm2_g[l]), sh2, sc2)
        xl = xl + g2 * conv_ffn(hl2, w_up[l], ffn_conv_w[l], ffn_conv_b[l], w_down[l], rows, GRID_W)

        if need_ctx:
            y_c = jnp.concatenate([ya_c, yb_c, yc_c], axis=-1).astype(ctx.dtype)
            xc = xc + cg1 * (y_c @ w_out[l])
            hc2 = modulate(rms_norm(xc, norm2_g[l]), csh2, csc2)
            xc = xc + cg2 * conv_ffn(hc2, w_up[l], ffn_conv_w[l], ffn_conv_b[l], w_down[l], 1, xc.shape[1])

    return rms_norm(xl, final_norm_g)
```

```python
import functools

import jax
import jax.numpy as jnp
from jax import lax
from jax.experimental import pallas as pl
from jax.experimental.pallas import tpu as pltpu

F32 = jnp.float32
BF16 = jnp.bfloat16
HIGHEST = lax.Precision.HIGHEST

EPS = 1e-6
GRID_W = 64
LANE = 128
SUBLANE = 8
A_DK = 128
A_HEADS_PER_STEP = 4
B_BW = 128
B_CONV = 4
RG_C = 8.0
C_HEADS = 4
C_RANK = 16
GLA_NORMALIZER = 16.0
TOKEN_BLOCK = 256
GLA_CHUNK = 128
GLA_SUB = 32
FFN_ROWS = 512
VMEM_LIMIT = 56 * 1024 * 1024


def _cp(*sem):
    return pltpu.CompilerParams(dimension_semantics=sem, vmem_limit_bytes=VMEM_LIMIT)


def _dot_nt(a, b):
    return lax.dot_general(a, b, (((1,), (1,)), ((), ())), preferred_element_type=F32)


def _dot_tn(a, b):
    return lax.dot_general(a, b, (((0,), (0,)), ((), ())), preferred_element_type=F32)


def _silu(v):
    return v * jax.nn.sigmoid(v)


def _ada_body(c_ref, w_ref, b_ref, o_ref):
    a = _silu(c_ref[...]).astype(BF16)
    o_ref[...] = jnp.dot(a, w_ref[...].astype(BF16), preferred_element_type=F32) + b_ref[...]


def _ada(c16, w_ada, b_ada, tn=512):
    depth, d, n = w_ada.shape
    rows = c16.shape[0]
    return pl.pallas_call(
        _ada_body,
        grid=(depth, n // tn),
        in_specs=[
            pl.BlockSpec((rows, d), lambda l, j: (0, 0)),
            pl.BlockSpec((None, d, tn), lambda l, j: (l, 0, j)),
            pl.BlockSpec((None, 1, tn), lambda l, j: (l, 0, j)),
        ],
        out_specs=pl.BlockSpec((None, rows, tn), lambda l, j: (l, 0, j)),
        out_shape=jax.ShapeDtypeStruct((depth, rows, n), F32),
        compiler_params=_cp("parallel", "parallel"),
        name="ada_mod",
    )(c16, w_ada, b_ada.reshape(depth, 1, n))


def _norm_mod_body(x_ref, g_ref, sh_ref, sc_ref, o_ref):
    x = x_ref[...]
    y = x * lax.rsqrt(jnp.mean(x * x, axis=-1, keepdims=True) + EPS) * g_ref[...]
    o_ref[...] = (y * (1.0 + sc_ref[...]) + sh_ref[...]).astype(o_ref.dtype)


def _norm_mod(x, g, shift, scale, nblk, n_ctx_blk):
    m, d = x.shape
    tr = TOKEN_BLOCK
    nb = shift.shape[0] - 1

    def mod_row(i):
        return jnp.where(i % nblk < n_ctx_blk, nb, i // nblk)

    return pl.pallas_call(
        _norm_mod_body,
        grid=(m // tr,),
        in_specs=[
            pl.BlockSpec((tr, d), lambda i: (i, 0)),
            pl.BlockSpec((1, d), lambda i: (0, 0)),
            pl.BlockSpec((None, 1, d), lambda i: (mod_row(i), 0, 0)),
            pl.BlockSpec((None, 1, d), lambda i: (mod_row(i), 0, 0)),
        ],
        out_specs=pl.BlockSpec((tr, d), lambda i: (i, 0)),
        out_shape=jax.ShapeDtypeStruct((m, d), BF16),
        compiler_params=_cp("parallel"),
        name="norm_mod",
    )(x, g.reshape(1, d), shift, scale)


def _mm_body(a_ref, b_ref, o_ref):
    o_ref[...] = jnp.dot(a_ref[...], b_ref[...], preferred_element_type=F32).astype(o_ref.dtype)


def _matmul(a, b, tm, tn, out_dtype=F32):
    m, k = a.shape
    n = b.shape[1]
    return pl.pallas_call(
        _mm_body,
        grid=(n // tn, m // tm),
        in_specs=[
            pl.BlockSpec((tm, k), lambda j, i: (i, 0)),
            pl.BlockSpec((k, tn), lambda j, i: (0, j)),
        ],
        out_specs=pl.BlockSpec((tm, tn), lambda j, i: (i, j)),
        out_shape=jax.ShapeDtypeStruct((m, n), out_dtype),
        compiler_params=_cp("parallel", "parallel"),
        name="proj_matmul",
    )(a, b)


def _mm_res_body(*refs, n_ops, ctx_rows, tiles_per_batch):
    a_refs = refs[:n_ops]
    b_refs = refs[n_ops:2 * n_ops]
    r_ref, gb_ref, gc_ref, o_ref = refs[2 * n_ops:]
    acc = jnp.dot(a_refs[0][...], b_refs[0][...], preferred_element_type=F32)
    for a_ref, b_ref in zip(a_refs[1:], b_refs[1:]):
        acc += jnp.dot(a_ref[...], b_ref[...], preferred_element_type=F32)
    tm = acc.shape[0]
    first = pl.program_id(1) % tiles_per_batch == 0
    row = lax.broadcasted_iota(jnp.int32, (tm, 1), 0)
    is_ctx = jnp.logical_and(first, row < ctx_rows)
    gate = jnp.where(is_ctx, gc_ref[...], gb_ref[...])
    o_ref[...] = r_ref[...] + gate * acc


def _matmul_res(a_list, b_list, resid, gate, tm, tn, rows_per_batch, ctx_rows):
    m, n = resid.shape
    nb = gate.shape[0] - 1
    tiles_per_batch = rows_per_batch // tm
    assert ctx_rows <= tm
    n_ops = len(a_list)
    in_specs = [pl.BlockSpec((tm, a.shape[1]), lambda j, i: (i, 0)) for a in a_list]
    in_specs += [pl.BlockSpec((b.shape[0], tn), lambda j, i: (0, j)) for b in b_list]
    in_specs += [
        pl.BlockSpec((tm, tn), lambda j, i: (i, j)),
        pl.BlockSpec((None, 1, tn), lambda j, i: (i // tiles_per_batch, 0, j)),
        pl.BlockSpec((None, 1, tn), lambda j, i: (nb, 0, j)),
    ]
    return pl.pallas_call(
        functools.partial(_mm_res_body, n_ops=n_ops, ctx_rows=ctx_rows, tiles_per_batch=tiles_per_batch),
        grid=(n // tn, m // tm),
        in_specs=in_specs,
        out_specs=pl.BlockSpec((tm, tn), lambda j, i: (i, j)),
        out_shape=jax.ShapeDtypeStruct((m, n), F32),
        compiler_params=_cp("parallel", "parallel"),
        name="res_matmul",
    )(*a_list, *b_list, resid, gate, gate)


def _gla_chunk(q, k, v, g, st, reverse):
    c, kdim = q.shape
    row = lax.broadcasted_iota(jnp.int32, (c, c), 0)
    col = lax.broadcasted_iota(jnp.int32, (c, c), 1)
    causal = (col >= row) if reverse else (col <= row)
    b = jnp.dot(causal.astype(F32), g, precision=HIGHEST, preferred_element_type=F32)
    edge = b[0:1] if reverse else b[c - 1:c]
    q_in = (q * jnp.exp(b)).astype(BF16)
    k_out = (k * jnp.exp(edge - b)).astype(BF16)
    vb = v.astype(BF16)
    o = _dot_nt(q_in, st.astype(BF16))
    st_new = st * jnp.exp(edge) + _dot_tn(vb, k_out)
    atts = []
    for i in range(c // GLA_SUB):
        r0 = i * GLA_SUB
        r1 = r0 + GLA_SUB
        if reverse:
            lo, hi = r0, c
            ref = b[r1:r1 + 1] if r1 < c else None
        else:
            lo, hi = 0, r1
            ref = b[r0 - 1:r0] if r0 > 0 else None
        eq = b[r0:r1] if ref is None else b[r0:r1] - ref
        ek = -b[lo:hi] if ref is None else ref - b[lo:hi]
        qi = (q[r0:r1] * jnp.exp(eq)).astype(BF16)
        kk = (k[lo:hi] * jnp.exp(ek)).astype(BF16)
        pieces = []
        if lo > 0:
            pieces.append(jnp.zeros((lo, kdim), BF16))
        pieces.append(kk)
        if hi < c:
            pieces.append(jnp.zeros((c - hi, kdim), BF16))
        if len(pieces) > 1:
            kk = jnp.concatenate(pieces, axis=0)
        att = jnp.where(causal[r0:r1], _dot_nt(qi, kk), 0.0)
        atts.append(att.astype(BF16))
    att = jnp.concatenate(atts, axis=0)
    o = o + jnp.dot(att, vb, preferred_element_type=F32)
    return o, st_new


def _seq_block(t, nblk, reverse):
    if not reverse:
        return t
    n_ctx = 1
    return jnp.where(t < n_ctx, n_ctx - 1 - t, nblk - 1 - (t - n_ctx))


def _hgrn_body(q_ref, z_ref, v_ref, lb_ref, *rest, reverse, finish):
    if finish:
        of_ref, g_ref, gain_ref, y_ref, st_ref = rest
    else:
        o_ref, st_ref = rest

    @pl.when(pl.program_id(2) == 0)
    def _():
        st_ref[...] = jnp.zeros_like(st_ref)

    tb = q_ref.shape[0]
    heads = q_ref.shape[1] // A_DK
    nchunk = tb // GLA_CHUNK
    order = range(nchunk - 1, -1, -1) if reverse else range(nchunk)
    for h in range(heads):
        cs = slice(h * A_DK, (h + 1) * A_DK)
        lb = lb_ref[:, cs]
        st = st_ref[h]
        for c in order:
            rs = slice(c * GLA_CHUNK, (c + 1) * GLA_CHUNK)
            q = q_ref[rs, cs]
            z = z_ref[rs, cs]
            qh = _silu(q) * (A_DK ** -0.5)
            logf = jnp.log(lb + (1.0 - lb) * jax.nn.sigmoid(z))
            key = (1.0 - lb) * jax.nn.sigmoid(-z)
            o, st = _gla_chunk(qh, key, v_ref[rs, cs], logf, st, reverse)
            if finish:
                o = o + of_ref[rs, cs]
                y = o * lax.rsqrt(jnp.mean(o * o, axis=-1, keepdims=True) + EPS)
                y_ref[rs, cs] = (y * gain_ref[:, cs] * _silu(g_ref[rs, cs])).astype(y_ref.dtype)
            else:
                o_ref[rs, cs] = o
        st_ref[h] = st


def _hgrn(proj, lb, nb, nblk, reverse, o_fwd=None, gain=None):
    m = proj.shape[0]
    width = lb.shape[1]
    hw = A_HEADS_PER_STEP * A_DK
    nhb = width // hw
    tb = TOKEN_BLOCK
    finish = o_fwd is not None

    def rows(b, hb, t):
        return b * nblk + _seq_block(t, nblk, reverse)

    def spec(section):
        return pl.BlockSpec((tb, hw), lambda b, hb, t: (rows(b, hb, t), section * nhb + hb))

    in_specs = [spec(0), spec(2 if reverse else 1), spec(3), pl.BlockSpec((1, hw), lambda b, hb, t: (0, hb))]
    args = [proj, proj, proj, lb]
    if finish:
        in_specs += [pl.BlockSpec((tb, hw), lambda b, hb, t: (rows(b, hb, t), hb)), spec(4),
                     pl.BlockSpec((1, hw), lambda b, hb, t: (0, hb))]
        args += [o_fwd, proj, gain]
        out_dtype = BF16
    else:
        out_dtype = F32
    return pl.pallas_call(
        functools.partial(_hgrn_body, reverse=reverse, finish=finish),
        grid=(nb, nhb, nblk),
        in_specs=in_specs,
        out_specs=pl.BlockSpec((tb, hw), lambda b, hb, t: (rows(b, hb, t), hb)),
        out_shape=jax.ShapeDtypeStruct((m, width), out_dtype),
        scratch_shapes=[pltpu.VMEM((A_HEADS_PER_STEP, A_DK, A_DK), F32)],
        compiler_params=_cp("parallel", "parallel", "arbitrary"),
        name="hgrn_bwd" if reverse else "hgrn_fwd",
    )(*args)


def _log_sigmoid(v):
    return jnp.minimum(v, 0.0) - jnp.log(1.0 + jnp.exp(-jnp.abs(v)))


def _glac_body(q_ref, k_ref, v_ref, a_ref, wa_ref, ba_ref, *rest, reverse, finish, q_scale):
    if finish:
        of_ref, g_ref, gain_ref, y_ref, st_ref = rest
    else:
        o_ref, st_ref = rest

    @pl.when(pl.program_id(1) == 0)
    def _():
        st_ref[...] = jnp.zeros_like(st_ref)

    tb = q_ref.shape[0]
    kp = q_ref.shape[1] // C_HEADS
    vw = v_ref.shape[1] // C_HEADS
    nchunk = tb // GLA_CHUNK
    order = range(nchunk - 1, -1, -1) if reverse else range(nchunk)
    states = [st_ref[h] for h in range(C_HEADS)]
    for c in order:
        rs = slice(c * GLA_CHUNK, (c + 1) * GLA_CHUNK)
        logits = jnp.dot(a_ref[rs, :], wa_ref[...], precision=HIGHEST, preferred_element_type=F32) + ba_ref[...]
        gate = _log_sigmoid(logits) * (1.0 / GLA_NORMALIZER)
        for h in range(C_HEADS):
            ks = slice(h * kp, (h + 1) * kp)
            vs = slice(h * vw, (h + 1) * vw)
            o, states[h] = _gla_chunk(q_ref[rs, ks] * q_scale, k_ref[rs, ks], v_ref[rs, vs], gate[:, ks],
                                      states[h], reverse)
            if finish:
                o = o + of_ref[rs, vs]
                y = o * lax.rsqrt(jnp.mean(o * o, axis=-1, keepdims=True) + EPS)
                y_ref[rs, vs] = (y * gain_ref[:, vs] * _silu(g_ref[rs, vs])).astype(y_ref.dtype)
            else:
                o_ref[rs, vs] = o
    for h in range(C_HEADS):
        st_ref[h] = states[h]


def _glac(proj, cols, wa_pad, ba_pad, q_scale, vw, nb, nblk, reverse, o_fwd=None, gain=None):
    m = proj.shape[0]
    kw = wa_pad.shape[1]
    tb = TOKEN_BLOCK
    finish = o_fwd is not None
    q_off, k_off, v_off, g_off, a_off = cols

    def rows(b, t):
        return b * nblk + _seq_block(t, nblk, reverse)

    def spec(width, off):
        assert (off * LANE) % width == 0
        return pl.BlockSpec((tb, width), lambda b, t: (rows(b, t), off * LANE // width))

    in_specs = [spec(kw, q_off), spec(kw, k_off), spec(vw, v_off), spec(LANE, a_off),
                pl.BlockSpec((LANE, kw), lambda b, t: (0, 0)), pl.BlockSpec((1, kw), lambda b, t: (0, 0))]
    args = [proj, proj, proj, proj, wa_pad, ba_pad]
    if finish:
        in_specs += [pl.BlockSpec((tb, vw), lambda b, t: (rows(b, t), 0)), spec(vw, g_off),
                     pl.BlockSpec((1, vw), lambda b, t: (0, 0))]
        args += [o_fwd, proj, gain]
        out_dtype = BF16
    else:
        out_dtype = F32
    return pl.pallas_call(
        functools.partial(_glac_body, reverse=reverse, finish=finish, q_scale=q_scale),
        grid=(nb, nblk),
        in_specs=in_specs,
        out_specs=pl.BlockSpec((tb, vw), lambda b, t: (rows(b, t), 0)),
        out_shape=jax.ShapeDtypeStruct((m, vw), out_dtype),
        scratch_shapes=[pltpu.VMEM((C_HEADS, vw // C_HEADS, kw // C_HEADS), F32)],
        compiler_params=_cp("parallel", "arbitrary"),
        name="gla_bwd" if reverse else "gla_fwd",
    )(*args)


def _lru_body(x_ref, cw_ref, cb_ref, w_ref, bias_ref, lam_ref, o_ref, *, nblk, n_ctx_blk):
    rb = TOKEN_BLOCK
    total = x_ref.shape[0]
    gw = x_ref.shape[1]
    rid = lax.broadcasted_iota(jnp.int32, (rb, gw), 0) & (SUBLANE - 1)
    cw = cw_ref[...]
    cb = cb_ref[...]

    def run(reverse):
        wsl = slice(2 * gw, 4 * gw) if reverse else slice(0, 2 * gw)
        lam = lam_ref[1:2] if reverse else lam_ref[0:1]
        neg_c_softplus = -RG_C * (jnp.maximum(-lam, 0.0) + jnp.log(1.0 + jnp.exp(-jnp.abs(lam))))
        w = w_ref[:, wsl]
        bias = bias_ref[:, wsl]

        def body(j, h):
            blk = _seq_block(j, nblk, reverse)
            start = pl.multiple_of(blk * rb, rb)
            cur = x_ref[pl.ds(start, rb), :]
            seg_start = jnp.logical_or(blk == 0, blk == n_ctx_blk)
            seg_end = jnp.logical_or(blk == n_ctx_blk - 1, blk == nblk - 1)
            pstart = pl.multiple_of(jnp.maximum(start - SUBLANE, 0), SUBLANE)
            nstart = pl.multiple_of(jnp.minimum(start + rb, total - SUBLANE), SUBLANE)
            prev8 = jnp.where(seg_start, 0.0, x_ref[pl.ds(pstart, SUBLANE), :])
            next8 = jnp.where(seg_end, 0.0, x_ref[pl.ds(nstart, SUBLANE), :])
            ext = jnp.concatenate([prev8, cur, next8], axis=0)
            n_ext = rb + 2 * SUBLANE

            def tap(shift):
                return pltpu.roll(ext, shift % n_ext, axis=0)[SUBLANE:SUBLANE + rb]

            xc = cb + cw[0:1] * tap(2) + cw[1:2] * tap(1) + cw[2:3] * cur + cw[3:4] * tap(-1)
            gates = jnp.dot(xc.astype(BF16), w, preferred_element_type=F32) + bias
            r = jax.nn.sigmoid(gates[:, :gw])
            i = jax.nn.sigmoid(gates[:, gw:])
            a = jnp.exp(neg_c_softplus * r)
            u = jnp.sqrt(jnp.maximum(1.0 - a * a, 0.0)) * (i * xc)
            for d in (1, 2, 4):
                if reverse:
                    keep = rid < SUBLANE - d
                    shift = rb - d
                else:
                    keep = rid >= d
                    shift = d
                a_s = jnp.where(keep, pltpu.roll(a, shift, axis=0), 1.0)
                u_s = jnp.where(keep, pltpu.roll(u, shift, axis=0), 0.0)
                u = a * u_s + u
                a = a * a_s
            ngroups = rb // SUBLANE
            out = [None] * ngroups
            for gi in (range(ngroups - 1, -1, -1) if reverse else range(ngroups)):
                gs = slice(gi * SUBLANE, (gi + 1) * SUBLANE)
                hg = u[gs] + a[gs] * h
                h = hg[0:1] if reverse else hg[SUBLANE - 1:SUBLANE]
                out[gi] = hg
            hblk = jnp.concatenate(out, axis=0)
            if reverse:
                o_ref[pl.ds(start, rb), :] = o_ref[pl.ds(start, rb), :] + hblk
            else:
                o_ref[pl.ds(start, rb), :] = hblk
            return h

        lax.fori_loop(0, nblk, body, jnp.zeros((1, gw), F32))

    run(False)
    run(True)


def _lru(proj, x_off, conv_w, conv_b, w_gates, b_gates, lam, nb, nblk, n_ctx_blk):
    m = proj.shape[0]
    nblocks = w_gates.shape[0]
    total = m // nb
    gw = B_BW
    return pl.pallas_call(
        functools.partial(_lru_body, nblk=nblk, n_ctx_blk=n_ctx_blk),
        grid=(nb, nblocks),
        in_specs=[
            pl.BlockSpec((total, gw), lambda b, n: (b, x_off + n)),
            pl.BlockSpec((B_CONV, gw), lambda b, n: (0, n)),
            pl.BlockSpec((1, gw), lambda b, n: (0, n)),
            pl.BlockSpec((None, gw, 4 * gw), lambda b, n: (n, 0, 0)),
            pl.BlockSpec((None, 1, 4 * gw), lambda b, n: (n, 0, 0)),
            pl.BlockSpec((2, gw), lambda b, n: (0, n)),
        ],
        out_specs=pl.BlockSpec((total, gw), lambda b, n: (b, n)),
        out_shape=jax.ShapeDtypeStruct((m, nblocks * gw), F32),
        compiler_params=_cp("parallel", "parallel"),
        name="rglru_scan",
    )(proj, conv_w, conv_b, w_gates, b_gates, lam)


def _gelu_tanh(v):
    return 0.5 * v * (1.0 + jnp.tanh(0.7978845608028654 * (v + 0.044715 * (v * v * v))))


def _lru_fin_body(h_ref, gate_ref, gain_ref, y_ref):
    h = h_ref[...]
    y = h * lax.rsqrt(jnp.mean(h * h, axis=-1, keepdims=True) + EPS) * gain_ref[...]
    y_ref[...] = (y * _gelu_tanh(gate_ref[...])).astype(y_ref.dtype)


def _lru_fin(h, proj, gate_off, gain):
    m, w = h.shape
    tr = TOKEN_BLOCK
    assert (gate_off * LANE) % w == 0
    return pl.pallas_call(
        _lru_fin_body,
        grid=(m // tr,),
        in_specs=[
            pl.BlockSpec((tr, w), lambda i: (i, 0)),
            pl.BlockSpec((tr, w), lambda i: (i, gate_off * LANE // w)),
            pl.BlockSpec((1, w), lambda i: (0, 0)),
        ],
        out_specs=pl.BlockSpec((tr, w), lambda i: (i, 0)),
        out_shape=jax.ShapeDtypeStruct((m, w), BF16),
        compiler_params=_cp("parallel"),
        name="rglru_finish",
    )(h, proj, gain)


def _conv_rows(up, mid, down, w, bias, u, width):
    n, tc = mid.shape
    left = w[3:4] * mid
    cent = w[4:5] * mid
    right = w[5:6] * mid
    if up is not None:
        left += w[0:1] * up
        cent += w[1:2] * up
        right += w[2:3] * up
    if down is not None:
        left += w[6:7] * down
        cent += w[7:8] * down
        right += w[8:9] * down
    cidx = lax.broadcasted_iota(jnp.int32, (n, tc), 0) & (width - 1)
    acc = cent + bias
    acc += jnp.where(cidx != 0, pltpu.roll(left, 1, axis=0), 0.0)
    acc += jnp.where(cidx != width - 1, pltpu.roll(right, n - 1, axis=0), 0.0)
    return (_silu(acc) * u).astype(BF16)


def _ffn_mid_body(u_ref, g_ref, w_ref, b_ref, o_ref, *, ctx_rows):
    total = u_ref.shape[0]
    w = w_ref[...]
    bias = b_ref[...]
    o_ref[0:ctx_rows, :] = _conv_rows(None, g_ref[0:ctx_rows, :], None, w, bias, u_ref[0:ctx_rows, :], ctx_rows)
    step = FFN_ROWS
    nsteps = (total - ctx_rows) // step
    zeros = jnp.zeros((GRID_W, u_ref.shape[1]), F32)

    def load(start, n):
        return g_ref[pl.ds(start, n), :]

    def emit(start, up, mid, down):
        o_ref[pl.ds(start, step), :] = _conv_rows(up, mid, down, w, bias, u_ref[pl.ds(start, step), :], GRID_W)

    first = ctx_rows
    last = ctx_rows + (nsteps - 1) * step
    if nsteps == 1:
        mid = load(first, step)
        emit(first, jnp.concatenate([zeros, mid[:step - GRID_W]], axis=0), mid,
             jnp.concatenate([mid[GRID_W:], zeros], axis=0))
        return
    mid = load(first, step)
    emit(first, jnp.concatenate([zeros, mid[:step - GRID_W]], axis=0), mid, load(first + GRID_W, step))

    def body(s, carry):
        start = pl.multiple_of(ctx_rows + s * step, GRID_W)
        emit(start, load(start - GRID_W, step), load(start, step), load(start + GRID_W, step))
        return carry

    lax.fori_loop(1, nsteps - 1, body, 0)
    mid = load(last, step)
    emit(last, load(last - GRID_W, step), mid, jnp.concatenate([mid[GRID_W:], zeros], axis=0))


def _ffn_mid(up, conv_w9, conv_b, nb, ctx_rows, tc=LANE):
    m, two_f = up.shape
    f = two_f // 2
    total = m // nb
    nct = f // tc
    return pl.pallas_call(
        functools.partial(_ffn_mid_body, ctx_rows=ctx_rows),
        grid=(nb, nct),
        in_specs=[
            pl.BlockSpec((total, tc), lambda b, j: (b, j)),
            pl.BlockSpec((total, tc), lambda b, j: (b, nct + j)),
            pl.BlockSpec((9, tc), lambda b, j: (0, j)),
            pl.BlockSpec((1, tc), lambda b, j: (0, j)),
        ],
        out_specs=pl.BlockSpec((total, tc), lambda b, j: (b, j)),
        out_shape=jax.ShapeDtypeStruct((m, f), BF16),
        compiler_params=_cp("parallel", "parallel"),
        name="ffn_conv_gate",
    )(up, up, conv_w9, conv_b)


def _final_norm_body(x_ref, g_ref, o_ref):
    x = x_ref[...]
    o_ref[...] = x * lax.rsqrt(jnp.mean(x * x, axis=-1, keepdims=True) + EPS) * g_ref[...]


def _final_norm(x, g, nb, nblk, n_ctx_blk, seq):
    d = x.shape[1]
    tr = TOKEN_BLOCK
    return pl.pallas_call(
        _final_norm_body,
        grid=(nb, seq // tr),
        in_specs=[
            pl.BlockSpec((tr, d), lambda b, i: (b * nblk + n_ctx_blk + i, 0)),
            pl.BlockSpec((1, d), lambda b, i: (0, 0)),
        ],
        out_specs=pl.BlockSpec((None, tr, d), lambda b, i: (b, i, 0)),
        out_shape=jax.ShapeDtypeStruct((nb, seq, d), F32),
        compiler_params=_cp("parallel", "parallel"),
        name="final_norm",
    )(x, g.reshape(1, d))


def _lower_bounds(logits):
    p = jax.nn.softmax(logits.astype(F32), axis=0)
    return jnp.clip(jnp.cumsum(p, axis=0) - p[0], 0.0, 1.0)


def _pad_heads(w, heads, width):
    lead = w.shape[:-1]
    wh = w.shape[-1] // heads
    w = w.reshape(lead + (heads, wh))
    w = jnp.pad(w, [(0, 0)] * len(lead) + [(0, 0), (0, width - wh)])
    return w.reshape(lead + (heads * width,))


def _pick_tile(total, candidates):
    for t in candidates:
        if total % t == 0:
            return t
    raise ValueError(f"no tile for {total}")


def kernel(x, c, ctx, c_ctx, w_ada, b_ada, norm1_g, w_in, hgrn_lb_fwd, hgrn_lb_bwd, hgrn_norm_g, lru_conv_w, lru_conv_b, lru_wr_fwd, lru_br_fwd, lru_wi_fwd, lru_bi_fwd, lru_lam_fwd, lru_wr_bwd, lru_br_bwd, lru_wi_bwd, lru_bi_bwd, lru_lam_bwd, lru_norm_g, gla_wa_fwd, gla_ba_fwd, gla_wa_bwd, gla_ba_bwd, gla_norm_g, w_out, norm2_g, w_up, ffn_conv_w, ffn_conv_b, w_down, final_norm_g):
    nb, seq, d = x.shape
    ctx_len = ctx.shape[1]
    depth = w_ada.shape[0]
    assert ctx_len == TOKEN_BLOCK and seq % FFN_ROWS == 0
    rows_per_batch = ctx_len + seq
    nblk = rows_per_batch // TOKEN_BLOCK
    n_ctx_blk = ctx_len // TOKEN_BLOCK
    m = nb * rows_per_batch

    a_w = hgrn_norm_g.shape[1]
    b_w = lru_norm_g.shape[1]
    c_w = gla_norm_g.shape[1]
    c_k = gla_ba_fwd.shape[1]
    c_dk = c_k // C_HEADS
    c_kp = -(-c_dk // LANE) * LANE
    n_blocks = b_w // B_BW

    ta, tb_, tck, tcv = a_w // LANE, b_w // LANE, C_HEADS * c_kp // LANE, c_w // LANE
    off_bx = 5 * ta
    off_ab = off_bx + tb_
    off_bg = -(-(off_ab + 1) // tb_) * tb_
    off_cq = -(-(off_bg + tb_) // tck) * tck
    off_ck = off_cq + tck
    off_cv = -(-(off_ck + tck) // tcv) * tcv
    off_cg = off_cv + tcv
    n_proj = (off_cg + tcv) * LANE
    tn_in = _pick_tile(n_proj, (1024, 512, 256, 128))
    n_proj = -(-n_proj // tn_in) * tn_in

    tm = _pick_tile(rows_per_batch, (768, 512, 256))

    xs = jnp.concatenate([ctx, x], axis=1).reshape(m, d)

    c16 = jnp.zeros((16, d), F32).at[:nb].set(c).at[nb].set(c_ctx)
    mods = _ada(c16, w_ada, b_ada)
    lbf_all = _lower_bounds(hgrn_lb_fwd)
    lbb_all = _lower_bounds(hgrn_lb_bwd)

    def place(cols, parts):
        out, pos = [], 0
        for off, arr in sorted(parts, key=lambda p: p[0]):
            start = off * LANE
            if start > pos:
                out.append(jnp.zeros((d, start - pos), arr.dtype))
            out.append(arr)
            pos = start + arr.shape[1]
        if pos < cols:
            out.append(jnp.zeros((d, cols - pos), parts[0][1].dtype))
        return jnp.concatenate(out, axis=1)

    for l in range(depth):
        mod = mods[l, :nb + 1].reshape(nb + 1, 6, 1, d)
        sh1, sc1, g1, sh2, sc2, g2 = (mod[:, i] for i in range(6))

        wl = w_in[l]
        o = 0
        sec = {}
        for name, size in (("a", 5 * a_w), ("bx", b_w), ("bg", b_w), ("cq", c_k), ("ck", c_k), ("cv", c_w),
                           ("cg", c_w), ("ab", 2 * C_RANK)):
            sec[name] = wl[:, o:o + size]
            o += size
        w_proj = place(n_proj, [
            (0, sec["a"]), (off_bx, sec["bx"]), (off_ab, sec["ab"]), (off_bg, sec["bg"]),
            (off_cq, _pad_heads(sec["cq"], C_HEADS, c_kp)), (off_ck, _pad_heads(sec["ck"], C_HEADS, c_kp)),
            (off_cv, sec["cv"]), (off_cg, sec["cg"]),
        ]).astype(BF16)

        h = _norm_mod(xs, norm1_g[l], sh1, sc1, nblk, n_ctx_blk)
        proj = _matmul(h, w_proj, tm, tn_in)

        lbf = lbf_all[l].reshape(1, a_w)
        lbb = lbb_all[l].reshape(1, a_w)
        oa = _hgrn(proj, lbf, nb, nblk, reverse=False)
        ya = _hgrn(proj, lbb, nb, nblk, reverse=True, o_fwd=oa, gain=hgrn_norm_g[l].reshape(1, a_w))

        w_gates = jnp.concatenate([lru_wr_fwd[l], lru_wi_fwd[l], lru_wr_bwd[l], lru_wi_bwd[l]], axis=-1).astype(BF16)
        b_gates = jnp.concatenate([t[l].reshape(n_blocks, 1, B_BW) for t in
                                   (lru_br_fwd, lru_bi_fwd, lru_br_bwd, lru_bi_bwd)], axis=-1)
        lam = jnp.stack([lru_lam_fwd[l], lru_lam_bwd[l]], axis=0)
        hb = _lru(proj, off_bx, lru_conv_w[l], lru_conv_b[l].reshape(1, b_w), w_gates, b_gates, lam,
                  nb, nblk, n_ctx_blk)
        yb = _lru_fin(hb, proj, off_bg, lru_norm_g[l].reshape(1, b_w))

        cols = (off_cq, off_ck, off_cv, off_cg, off_ab)
        kw = C_HEADS * c_kp

        def gate_params(wa, ba, row0):
            wp = jnp.zeros((LANE, kw), F32).at[row0:row0 + C_RANK].set(_pad_heads(wa, C_HEADS, c_kp))
            return wp, _pad_heads(ba, C_HEADS, c_kp).reshape(1, kw)

        waf, baf = gate_params(gla_wa_fwd[l], gla_ba_fwd[l], 0)
        wab, bab = gate_params(gla_wa_bwd[l], gla_ba_bwd[l], C_RANK)
        q_scale = c_dk ** -0.5
        gain_c = gla_norm_g[l].reshape(1, c_w)
        oc = _glac(proj, cols, waf, baf, q_scale, c_w, nb, nblk, reverse=False)
        yc = _glac(proj, cols, wab, bab, q_scale, c_w, nb, nblk, reverse=True, o_fwd=oc, gain=gain_c)

        wo = w_out[l]
        wo_parts = [wo[:a_w].astype(BF16), wo[a_w:a_w + b_w].astype(BF16), wo[a_w + b_w:].astype(BF16)]
        xs = _matmul_res([ya, yb, yc], wo_parts, xs, g1, tm, 1024, rows_per_batch, ctx_len)

        h2 = _norm_mod(xs, norm2_g[l], sh2, sc2, nblk, n_ctx_blk)
        up = _matmul(h2, w_up[l].astype(BF16), tm, 1024)
        act = _ffn_mid(up, ffn_conv_w[l].reshape(9, -1), ffn_conv_b[l].reshape(1, -1), nb, ctx_len)
        xs = _matmul_res([act], [w_down[l].astype(BF16)], xs, g2, tm, 1024, rows_per_batch, ctx_len)

    return _final_norm(xs, final_norm_g, nb, nblk, n_ctx_blk, seq)
```

```python
import functools

import jax
import jax.numpy as jnp
from jax import lax
from jax.experimental import pallas as pl
from jax.experimental.pallas import tpu as pltpu

F32 = jnp.float32
BF16 = jnp.bfloat16
HIGHEST = lax.Precision.HIGHEST

EPS = 1e-6
GRID_W = 64
LANE = 128
SUBLANE = 8
A_DK = 128
A_HEADS_PER_STEP = 4
B_BW = 128
B_CONV = 4
RG_C = 8.0
C_HEADS = 4
C_RANK = 16
GLA_NORMALIZER = 16.0
TOKEN_BLOCK = 256
GLA_CHUNK = 128
GLA_SUB = 32
LOG2 = 0.6931471805599453
TINY = 1e-30
SAFE_HALF_DECAY = 60.0
FFN_ROWS = 512
VMEM_LIMIT = 56 * 1024 * 1024


def _cp(*sem):
    return pltpu.CompilerParams(dimension_semantics=sem, vmem_limit_bytes=VMEM_LIMIT)


def _dot_nt(a, b):
    return lax.dot_general(a, b, (((1,), (1,)), ((), ())), preferred_element_type=F32)


def _dot_tn(a, b):
    return lax.dot_general(a, b, (((0,), (0,)), ((), ())), preferred_element_type=F32)


def _silu(v):
    return v * jax.nn.sigmoid(v)


def _ada_body(c_ref, w_ref, b_ref, o_ref):
    a = _silu(c_ref[...]).astype(BF16)
    o_ref[...] = jnp.dot(a, w_ref[...].astype(BF16), preferred_element_type=F32) + b_ref[...]


def _ada(c16, w_ada, b_ada, tn=512):
    depth, d, n = w_ada.shape
    rows = c16.shape[0]
    return pl.pallas_call(
        _ada_body,
        grid=(depth, n // tn),
        in_specs=[
            pl.BlockSpec((rows, d), lambda l, j: (0, 0)),
            pl.BlockSpec((None, d, tn), lambda l, j: (l, 0, j)),
            pl.BlockSpec((None, 1, tn), lambda l, j: (l, 0, j)),
        ],
        out_specs=pl.BlockSpec((None, rows, tn), lambda l, j: (l, 0, j)),
        out_shape=jax.ShapeDtypeStruct((depth, rows, n), F32),
        compiler_params=_cp("parallel", "parallel"),
        name="ada_mod",
    )(c16, w_ada, b_ada.reshape(depth, 1, n))


def _norm_mod_body(x_ref, g_ref, sh_ref, sc_ref, o_ref):
    x = x_ref[...]
    y = x * lax.rsqrt(jnp.mean(x * x, axis=-1, keepdims=True) + EPS) * g_ref[...]
    o_ref[...] = (y * (1.0 + sc_ref[...]) + sh_ref[...]).astype(o_ref.dtype)


def _norm_mod(x, g, shift, scale, nblk, n_ctx_blk):
    m, d = x.shape
    tr = TOKEN_BLOCK
    nb = shift.shape[0] - 1

    def mod_row(i):
        return jnp.where(i % nblk < n_ctx_blk, nb, i // nblk)

    return pl.pallas_call(
        _norm_mod_body,
        grid=(m // tr,),
        in_specs=[
            pl.BlockSpec((tr, d), lambda i: (i, 0)),
            pl.BlockSpec((1, d), lambda i: (0, 0)),
            pl.BlockSpec((None, 1, d), lambda i: (mod_row(i), 0, 0)),
            pl.BlockSpec((None, 1, d), lambda i: (mod_row(i), 0, 0)),
        ],
        out_specs=pl.BlockSpec((tr, d), lambda i: (i, 0)),
        out_shape=jax.ShapeDtypeStruct((m, d), BF16),
        compiler_params=_cp("parallel"),
        name="norm_mod",
    )(x, g.reshape(1, d), shift, scale)


def _mm_body(a_ref, b_ref, o_ref):
    o_ref[...] = jnp.dot(a_ref[...], b_ref[...], preferred_element_type=F32).astype(o_ref.dtype)


def _matmul(a, b, tm, tn, out_dtype=F32):
    m, k = a.shape
    n = b.shape[1]
    return pl.pallas_call(
        _mm_body,
        grid=(n // tn, m // tm),
        in_specs=[
            pl.BlockSpec((tm, k), lambda j, i: (i, 0)),
            pl.BlockSpec((k, tn), lambda j, i: (0, j)),
        ],
        out_specs=pl.BlockSpec((tm, tn), lambda j, i: (i, j)),
        out_shape=jax.ShapeDtypeStruct((m, n), out_dtype),
        compiler_params=_cp("parallel", "parallel"),
        name="proj_matmul",
    )(a, b)


def _mm_res_body(*refs, n_ops, ctx_rows, tiles_per_batch):
    a_refs = refs[:n_ops]
    b_refs = refs[n_ops:2 * n_ops]
    r_ref, gb_ref, gc_ref, o_ref = refs[2 * n_ops:]
    acc = jnp.dot(a_refs[0][...], b_refs[0][...], preferred_element_type=F32)
    for a_ref, b_ref in zip(a_refs[1:], b_refs[1:]):
        acc += jnp.dot(a_ref[...], b_ref[...], preferred_element_type=F32)
    tm = acc.shape[0]
    first = pl.program_id(1) % tiles_per_batch == 0
    row = lax.broadcasted_iota(jnp.int32, (tm, 1), 0)
    is_ctx = jnp.logical_and(first, row < ctx_rows)
    gate = jnp.where(is_ctx, gc_ref[...], gb_ref[...])
    o_ref[...] = r_ref[...] + gate * acc


def _matmul_res(a_list, b_list, resid, gate, tm, tn, rows_per_batch, ctx_rows):
    m, n = resid.shape
    nb = gate.shape[0] - 1
    tiles_per_batch = rows_per_batch // tm
    assert ctx_rows <= tm
    n_ops = len(a_list)
    in_specs = [pl.BlockSpec((tm, a.shape[1]), lambda j, i: (i, 0)) for a in a_list]
    in_specs += [pl.BlockSpec((b.shape[0], tn), lambda j, i: (0, j)) for b in b_list]
    in_specs += [
        pl.BlockSpec((tm, tn), lambda j, i: (i, j)),
        pl.BlockSpec((None, 1, tn), lambda j, i: (i // tiles_per_batch, 0, j)),
        pl.BlockSpec((None, 1, tn), lambda j, i: (nb, 0, j)),
    ]
    return pl.pallas_call(
        functools.partial(_mm_res_body, n_ops=n_ops, ctx_rows=ctx_rows, tiles_per_batch=tiles_per_batch),
        grid=(n // tn, m // tm),
        in_specs=in_specs,
        out_specs=pl.BlockSpec((tm, tn), lambda j, i: (i, j)),
        out_shape=jax.ShapeDtypeStruct((m, n), F32),
        compiler_params=_cp("parallel", "parallel"),
        name="res_matmul",
    )(*a_list, *b_list, resid, gate, gate)


def _split3(g):
    hi = g.astype(BF16)
    r = g - hi.astype(F32)
    mid = r.astype(BF16)
    lo = (r - mid.astype(F32)).astype(BF16)
    return jnp.concatenate([hi, mid, lo], axis=1)


def _blocks(rows_list):
    return jnp.concatenate([jnp.broadcast_to(r, (GLA_SUB, r.shape[1])) for r in rows_list], axis=0)


def _window_decay_is_large(logf_bound):
    t, w = logf_bound.shape
    sums = jnp.sum(logf_bound.reshape(t // GLA_SUB, GLA_SUB, w), axis=1)
    return jnp.min(sums) < -2.0 * SAFE_HALF_DECAY


def _diag_scores_safe(q, k, g, reverse):
    c = q.shape[0]
    kdim = q.shape[1]
    row = lax.broadcasted_iota(jnp.int32, (c, c), 0)
    col = lax.broadcasted_iota(jnp.int32, (c, c), 1)
    prow = lax.broadcasted_iota(jnp.int32, (c, kdim), 0)
    g3 = _split3(g)
    kb = k.astype(BF16)
    scores = jnp.where(row == col, _dot_nt(q.astype(BF16), kb), 0.0)
    w = 1
    while w < GLA_SUB:
        inblock = 2 * w - 1
        mid = (row & ~inblock) + w
        if reverse:
            take = ((col >= row) & (col < mid)) | ((row >= mid) & (col >= mid) & (col < row))
            is_query = (prow & inblock) < w
        else:
            take = ((col >= mid) & (col <= row)) | ((row < mid) & (col > row) & (col < mid))
            is_query = (prow & inblock) >= w
        ee = jnp.dot(jnp.where(take, 1.0, 0.0).astype(BF16), g3, preferred_element_type=F32)
        factor = jnp.exp(ee[:, :kdim] + ee[:, kdim:2 * kdim] + ee[:, 2 * kdim:])
        qm = jnp.where(is_query, q * factor, 0.0).astype(BF16)
        km = jnp.where(is_query, 0.0, k * factor).astype(BF16)
        same = (row & ~inblock) == (col & ~inblock)
        scores = scores + jnp.where(same, _dot_nt(qm, km), 0.0)
        w *= 2
    return scores


def _gla_chunks(qs, ks, vs, gs, sts, reverse, robust, safe_diags):
    c = qs[0].shape[0]
    nsub = c // GLA_SUB
    row = lax.broadcasted_iota(jnp.int32, (c, c), 0)
    col = lax.broadcasted_iota(jnp.int32, (c, c), 1)
    tri = jnp.where((col >= row) if reverse else (col <= row), 1.0, 0.0).astype(BF16)
    srow = lax.broadcasted_iota(jnp.int32, (GLA_SUB, c), 0)
    scol = lax.broadcasted_iota(jnp.int32, (GLA_SUB, c), 1)
    order = list(range(nsub - 1, -1, -1)) if reverse else list(range(nsub))

    bs = []
    for g in gs:
        kdim = g.shape[1]
        bb = jnp.dot(tri, _split3(g), preferred_element_type=F32)
        bs.append(bb[:, :kdim] + bb[:, kdim:2 * kdim] + bb[:, 2 * kdim:])

    work = []
    for q, k, b in zip(qs, ks, bs):
        zero = jnp.zeros((1, q.shape[1]), F32)
        entry, leave = [None] * nsub, [None] * nsub
        for i in range(nsub):
            r0, r1 = i * GLA_SUB, (i + 1) * GLA_SUB
            if reverse:
                entry[i] = b[r1:r1 + 1] if r1 < c else zero
                leave[i] = b[r0:r0 + 1]
            else:
                entry[i] = b[r0 - 1:r0] if r0 > 0 else zero
                leave[i] = b[r1 - 1:r1]
        edge = leave[order[-1]]
        qi = q * jnp.exp(b - _blocks(entry))
        kh = k * jnp.exp(_blocks(leave) - b)
        half = _blocks([jnp.exp(0.5 * (entry[i] - leave[i])) for i in range(nsub)])
        q_in = (qi * _blocks([jnp.exp(e) for e in entry])).astype(BF16)
        k_out = (kh * _blocks([jnp.exp(edge - x) for x in leave])).astype(BF16)
        qd = (qi * half).astype(BF16)
        kd = (kh * half).astype(BF16)
        pairs = []
        for pos, i in enumerate(order):
            for j in order[:pos]:
                r0 = i * GLA_SUB
                pairs.append((i, j, (qi[r0:r0 + GLA_SUB] * jnp.exp(entry[i] - leave[j])).astype(BF16)))
        work.append((q_in, k_out, qd, kd, kh.astype(BF16), pairs, edge))

    atts = []
    for (q_in, k_out, qd, kd, khb, pairs, edge), safe_diag in zip(work, safe_diags):
        diag = jnp.where(robust, safe_diag, _dot_nt(qd, kd))
        off = _dot_nt(jnp.concatenate([p[2] for p in pairs], axis=0), khb)
        blocks = []
        for i in range(nsub):
            r0 = i * GLA_SUB
            inside = (scol >= r0) & (scol < r0 + GLA_SUB)
            keep = inside & ((scol - r0 >= srow) if reverse else (scol - r0 <= srow))
            att = jnp.where(keep, diag[r0:r0 + GLA_SUB], 0.0)
            for n, (pi, pj, _) in enumerate(pairs):
                if pi == i:
                    c0 = pj * GLA_SUB
                    att = jnp.where((scol >= c0) & (scol < c0 + GLA_SUB), off[n * GLA_SUB:(n + 1) * GLA_SUB], att)
            blocks.append(att.astype(BF16))
        atts.append(jnp.concatenate(blocks, axis=0))

    outs, new_sts = [], []
    for (q_in, k_out, qd, kd, khb, pairs, edge), att, v, st in zip(work, atts, vs, sts):
        vb = v.astype(BF16)
        outs.append(jnp.dot(att, vb, preferred_element_type=F32) + _dot_nt(q_in, st.astype(BF16)))
        new_sts.append(st * jnp.exp(edge) + _dot_tn(vb, k_out))
    return outs, new_sts


def _seq_block(t, nblk, reverse):
    if not reverse:
        return t
    n_ctx = 1
    return jnp.where(t < n_ctx, n_ctx - 1 - t, nblk - 1 - (t - n_ctx))


def _hgrn_body(q_ref, z_ref, v_ref, lb_ref, *rest, reverse, finish):
    if finish:
        of_ref, g_ref, gain_ref, y_ref, st_ref, diag_ref = rest
    else:
        o_ref, st_ref, diag_ref = rest

    @pl.when(pl.program_id(2) == 0)
    def _():
        st_ref[...] = jnp.zeros_like(st_ref)
        diag_ref[...] = jnp.zeros_like(diag_ref)

    tb = q_ref.shape[0]
    heads = q_ref.shape[1] // A_DK
    nchunk = tb // GLA_CHUNK
    order = range(nchunk - 1, -1, -1) if reverse else range(nchunk)
    lb = lb_ref[...]
    cols = [slice(h * A_DK, (h + 1) * A_DK) for h in range(heads)]

    def prep(rs):
        qh = _silu(q_ref[rs, :]) * (A_DK ** -0.5)
        sig = jax.nn.sigmoid(z_ref[rs, :])
        key = (1.0 - lb) * (1.0 - sig)
        logf = jnp.log(lb + (1.0 - lb) * sig)
        return qh, key, logf

    robust = _window_decay_is_large(jnp.minimum(z_ref[...], 0.0) - LOG2)

    @pl.when(robust)
    def _():
        for c in order:
            rs = slice(c * GLA_CHUNK, (c + 1) * GLA_CHUNK)
            qh, key, logf = prep(rs)
            for h, cs in enumerate(cols):
                diag_ref[c * heads + h] = _diag_scores_safe(qh[:, cs], key[:, cs], logf[:, cs], reverse)

    sts = [st_ref[h] for h in range(heads)]
    for c in order:
        rs = slice(c * GLA_CHUNK, (c + 1) * GLA_CHUNK)
        qh, key, logf = prep(rs)
        v = v_ref[rs, :]
        outs, sts = _gla_chunks([qh[:, cs] for cs in cols], [key[:, cs] for cs in cols], [v[:, cs] for cs in cols],
                                [logf[:, cs] for cs in cols], sts, reverse, robust,
                                [diag_ref[c * heads + h] for h in range(heads)])
        for cs, o in zip(cols, outs):
            if finish:
                o = o + of_ref[rs, cs]
                y = o * lax.rsqrt(jnp.mean(o * o, axis=-1, keepdims=True) + EPS)
                y_ref[rs, cs] = (y * gain_ref[:, cs] * _silu(g_ref[rs, cs])).astype(y_ref.dtype)
            else:
                o_ref[rs, cs] = o
    for h in range(heads):
        st_ref[h] = sts[h]


def _hgrn(proj, lb, nb, nblk, reverse, o_fwd=None, gain=None):
    m = proj.shape[0]
    width = lb.shape[1]
    hw = A_HEADS_PER_STEP * A_DK
    nhb = width // hw
    tb = TOKEN_BLOCK
    finish = o_fwd is not None

    def rows(b, hb, t):
        return b * nblk + _seq_block(t, nblk, reverse)

    def spec(section):
        return pl.BlockSpec((tb, hw), lambda b, hb, t: (rows(b, hb, t), section * nhb + hb))

    in_specs = [spec(0), spec(2 if reverse else 1), spec(3), pl.BlockSpec((1, hw), lambda b, hb, t: (0, hb))]
    args = [proj, proj, proj, lb]
    if finish:
        in_specs += [pl.BlockSpec((tb, hw), lambda b, hb, t: (rows(b, hb, t), hb)), spec(4),
                     pl.BlockSpec((1, hw), lambda b, hb, t: (0, hb))]
        args += [o_fwd, proj, gain]
        out_dtype = BF16
    else:
        out_dtype = F32
    return pl.pallas_call(
        functools.partial(_hgrn_body, reverse=reverse, finish=finish),
        grid=(nb, nhb, nblk),
        in_specs=in_specs,
        out_specs=pl.BlockSpec((tb, hw), lambda b, hb, t: (rows(b, hb, t), hb)),
        out_shape=jax.ShapeDtypeStruct((m, width), out_dtype),
        scratch_shapes=[pltpu.VMEM((A_HEADS_PER_STEP, A_DK, A_DK), F32),
                        pltpu.VMEM((A_HEADS_PER_STEP * (tb // GLA_CHUNK), GLA_CHUNK, GLA_CHUNK), F32)],
        compiler_params=_cp("parallel", "parallel", "arbitrary"),
        name="hgrn_bwd" if reverse else "hgrn_fwd",
    )(*args)


def _log_sigmoid(v):
    return jnp.minimum(v, 0.0) - jnp.log(1.0 + jnp.exp(-jnp.abs(v)))


def _glac_body(q_ref, k_ref, v_ref, a_ref, wa_ref, ba_ref, *rest, reverse, finish, q_scale):
    if finish:
        of_ref, g_ref, gain_ref, y_ref, st_ref, diag_ref = rest
    else:
        o_ref, st_ref, diag_ref = rest

    @pl.when(pl.program_id(1) == 0)
    def _():
        st_ref[...] = jnp.zeros_like(st_ref)
        diag_ref[...] = jnp.zeros_like(diag_ref)

    tb = q_ref.shape[0]
    kp = q_ref.shape[1] // C_HEADS
    vw = v_ref.shape[1] // C_HEADS
    nchunk = tb // GLA_CHUNK
    order = range(nchunk - 1, -1, -1) if reverse else range(nchunk)
    a = a_ref[...]
    a_hi = a.astype(BF16)
    a_lo = (a - a_hi.astype(F32)).astype(BF16)
    logits = jnp.dot(jnp.concatenate([a_hi, a_lo, a_hi], axis=1), wa_ref[...],
                     preferred_element_type=F32) + ba_ref[...]
    robust = _window_decay_is_large((jnp.minimum(logits, 0.0) - LOG2) * (1.0 / GLA_NORMALIZER))
    kcols = [slice(h * kp, (h + 1) * kp) for h in range(C_HEADS)]
    vcols = [slice(h * vw, (h + 1) * vw) for h in range(C_HEADS)]

    def gate_of(rs):
        return _log_sigmoid(logits[rs]) * (1.0 / GLA_NORMALIZER)

    @pl.when(robust)
    def _():
        for c in order:
            rs = slice(c * GLA_CHUNK, (c + 1) * GLA_CHUNK)
            gate = gate_of(rs)
            for h, ks in enumerate(kcols):
                diag_ref[c * C_HEADS + h] = _diag_scores_safe(q_ref[rs, ks] * q_scale, k_ref[rs, ks], gate[:, ks],
                                                              reverse)

    states = [st_ref[h] for h in range(C_HEADS)]
    for c in order:
        rs = slice(c * GLA_CHUNK, (c + 1) * GLA_CHUNK)
        gate = gate_of(rs)
        outs, states = _gla_chunks([q_ref[rs, ks] * q_scale for ks in kcols], [k_ref[rs, ks] for ks in kcols],
                                   [v_ref[rs, vs] for vs in vcols], [gate[:, ks] for ks in kcols], states, reverse,
                                   robust, [diag_ref[c * C_HEADS + h] for h in range(C_HEADS)])
        for vs, o in zip(vcols, outs):
            if finish:
                o = o + of_ref[rs, vs]
                y = o * lax.rsqrt(jnp.mean(o * o, axis=-1, keepdims=True) + EPS)
                y_ref[rs, vs] = (y * gain_ref[:, vs] * _silu(g_ref[rs, vs])).astype(y_ref.dtype)
            else:
                o_ref[rs, vs] = o
    for h in range(C_HEADS):
        st_ref[h] = states[h]


def _glac(proj, cols, wa_pad, ba_pad, q_scale, vw, nb, nblk, reverse, o_fwd=None, gain=None):
    m = proj.shape[0]
    kw = wa_pad.shape[1]
    tb = TOKEN_BLOCK
    finish = o_fwd is not None
    q_off, k_off, v_off, g_off, a_off = cols

    def rows(b, t):
        return b * nblk + _seq_block(t, nblk, reverse)

    def spec(width, off):
        assert (off * LANE) % width == 0
        return pl.BlockSpec((tb, width), lambda b, t: (rows(b, t), off * LANE // width))

    in_specs = [spec(kw, q_off), spec(kw, k_off), spec(vw, v_off), spec(LANE, a_off),
                pl.BlockSpec((3 * LANE, kw), lambda b, t: (0, 0)), pl.BlockSpec((1, kw), lambda b, t: (0, 0))]
    args = [proj, proj, proj, proj, wa_pad, ba_pad]
    if finish:
        in_specs += [pl.BlockSpec((tb, vw), lambda b, t: (rows(b, t), 0)), spec(vw, g_off),
                     pl.BlockSpec((1, vw), lambda b, t: (0, 0))]
        args += [o_fwd, proj, gain]
        out_dtype = BF16
    else:
        out_dtype = F32
    return pl.pallas_call(
        functools.partial(_glac_body, reverse=reverse, finish=finish, q_scale=q_scale),
        grid=(nb, nblk),
        in_specs=in_specs,
        out_specs=pl.BlockSpec((tb, vw), lambda b, t: (rows(b, t), 0)),
        out_shape=jax.ShapeDtypeStruct((m, vw), out_dtype),
        scratch_shapes=[pltpu.VMEM((C_HEADS, vw // C_HEADS, kw // C_HEADS), F32),
                        pltpu.VMEM((C_HEADS * (tb // GLA_CHUNK), GLA_CHUNK, GLA_CHUNK), F32)],
        compiler_params=_cp("parallel", "arbitrary"),
        name="gla_bwd" if reverse else "gla_fwd",
    )(*args)


def _lru_body(x_ref, cw_ref, cb_ref, w_ref, bias_ref, lam_ref, o_ref, xc_ref, *, nblk, n_ctx_blk):
    rb = TOKEN_BLOCK
    total = x_ref.shape[0]
    gw = x_ref.shape[1]
    rid = lax.broadcasted_iota(jnp.int32, (SUBLANE, gw), 0)
    cw = cw_ref[...]
    cb = cb_ref[...]

    def conv(blk, start):
        cur = x_ref[pl.ds(start, rb), :]
        seg_start = jnp.logical_or(blk == 0, blk == n_ctx_blk)
        seg_end = jnp.logical_or(blk == n_ctx_blk - 1, blk == nblk - 1)
        pstart = pl.multiple_of(jnp.maximum(start - SUBLANE, 0), SUBLANE)
        nstart = pl.multiple_of(jnp.minimum(start + rb, total - SUBLANE), SUBLANE)
        prev8 = jnp.where(seg_start, 0.0, x_ref[pl.ds(pstart, SUBLANE), :])
        next8 = jnp.where(seg_end, 0.0, x_ref[pl.ds(nstart, SUBLANE), :])
        ext = jnp.concatenate([prev8, cur, next8], axis=0)
        n_ext = rb + 2 * SUBLANE

        def tap(shift):
            return pltpu.roll(ext, shift % n_ext, axis=0)[SUBLANE:SUBLANE + rb]

        return cb + cw[0:1] * tap(2) + cw[1:2] * tap(1) + cw[2:3] * cur + cw[3:4] * tap(-1)

    def run(reverse):
        wsl = slice(2 * gw, 4 * gw) if reverse else slice(0, 2 * gw)
        lam = lam_ref[1:2] if reverse else lam_ref[0:1]
        neg_c_softplus = -RG_C * (jnp.maximum(-lam, 0.0) + jnp.log(1.0 + jnp.exp(-jnp.abs(lam))))
        w = w_ref[:, wsl]
        bias = bias_ref[:, wsl]

        def stage(j):
            blk = _seq_block(j, nblk, reverse)
            start = pl.multiple_of(blk * rb, rb)
            if reverse:
                xc = xc_ref[pl.ds(start, rb), :]
            else:
                xc = conv(blk, start)
                xc_ref[pl.ds(start, rb), :] = xc
            return xc, jnp.dot(xc.astype(BF16), w, preferred_element_type=F32) + bias

        def body(j, carry):
            if reverse:
                h, xc, gates = carry
                xc_next, gates_next = stage(jnp.minimum(j + 1, nblk - 1))
            else:
                h = carry
                xc, gates = stage(j)
            blk = _seq_block(j, nblk, reverse)
            start = pl.multiple_of(blk * rb, rb)
            r = jax.nn.sigmoid(gates[:, :gw])
            i = jax.nn.sigmoid(gates[:, gw:])
            a = jnp.exp(neg_c_softplus * r)
            y = 1.0 - a * a
            u = (y * lax.rsqrt(jnp.maximum(y, TINY))) * (i * xc)
            ngroups = rb // SUBLANE
            order = range(ngroups - 1, -1, -1) if reverse else range(ngroups)
            last = slice(0, 1) if reverse else slice(SUBLANE - 1, SUBLANE)
            scanned = [None] * ngroups
            for gi in order:
                gs = slice(gi * SUBLANE, (gi + 1) * SUBLANE)
                ag, ug = a[gs], u[gs]
                for d in (1, 2, 4):
                    keep = (rid < SUBLANE - d) if reverse else (rid >= d)
                    shift = SUBLANE - d if reverse else d
                    a_s = jnp.where(keep, pltpu.roll(ag, shift, axis=0), 1.0)
                    u_s = jnp.where(keep, pltpu.roll(ug, shift, axis=0), 0.0)
                    ug = ag * u_s + ug
                    ag = ag * a_s
                scanned[gi] = (ag, ug)
            carry = [None] * ngroups
            for gi in order:
                ag, ug = scanned[gi]
                carry[gi] = h
                h = ag[last] * h + ug[last]
            hblk = jnp.concatenate([scanned[gi][1] + scanned[gi][0] * carry[gi] for gi in range(ngroups)], axis=0)
            if reverse:
                o_ref[pl.ds(start, rb), :] = o_ref[pl.ds(start, rb), :] + hblk
            else:
                o_ref[pl.ds(start, rb), :] = hblk
            return (h, xc_next, gates_next) if reverse else h

        h0 = jnp.zeros((1, gw), F32)
        lax.fori_loop(0, nblk, body, (h0,) + stage(0) if reverse else h0)

    run(False)
    run(True)


def _lru(proj, x_off, conv_w, conv_b, w_gates, b_gates, lam, nb, nblk, n_ctx_blk):
    m = proj.shape[0]
    nblocks = w_gates.shape[0]
    total = m // nb
    gw = B_BW
    return pl.pallas_call(
        functools.partial(_lru_body, nblk=nblk, n_ctx_blk=n_ctx_blk),
        grid=(nb, nblocks),
        in_specs=[
            pl.BlockSpec((total, gw), lambda b, n: (b, x_off + n)),
            pl.BlockSpec((B_CONV, gw), lambda b, n: (0, n)),
            pl.BlockSpec((1, gw), lambda b, n: (0, n)),
            pl.BlockSpec((None, gw, 4 * gw), lambda b, n: (n, 0, 0)),
            pl.BlockSpec((None, 1, 4 * gw), lambda b, n: (n, 0, 0)),
            pl.BlockSpec((2, gw), lambda b, n: (0, n)),
        ],
        out_specs=pl.BlockSpec((total, gw), lambda b, n: (b, n)),
        out_shape=jax.ShapeDtypeStruct((m, nblocks * gw), F32),
        scratch_shapes=[pltpu.VMEM((total, gw), F32)],
        compiler_params=_cp("parallel", "parallel"),
        name="rglru_scan",
    )(proj, conv_w, conv_b, w_gates, b_gates, lam)


def _gelu_tanh(v):
    return 0.5 * v * (1.0 + jnp.tanh(0.7978845608028654 * (v + 0.044715 * (v * v * v))))


def _lru_fin_body(h_ref, gate_ref, gain_ref, y_ref):
    h = h_ref[...]
    y = h * lax.rsqrt(jnp.mean(h * h, axis=-1, keepdims=True) + EPS) * gain_ref[...]
    y_ref[...] = (y * _gelu_tanh(gate_ref[...])).astype(y_ref.dtype)


def _lru_fin(h, proj, gate_off, gain):
    m, w = h.shape
    tr = TOKEN_BLOCK
    assert (gate_off * LANE) % w == 0
    return pl.pallas_call(
        _lru_fin_body,
        grid=(m // tr,),
        in_specs=[
            pl.BlockSpec((tr, w), lambda i: (i, 0)),
            pl.BlockSpec((tr, w), lambda i: (i, gate_off * LANE // w)),
            pl.BlockSpec((1, w), lambda i: (0, 0)),
        ],
        out_specs=pl.BlockSpec((tr, w), lambda i: (i, 0)),
        out_shape=jax.ShapeDtypeStruct((m, w), BF16),
        compiler_params=_cp("parallel"),
        name="rglru_finish",
    )(h, proj, gain)


def _conv_rows(up, mid, down, w, bias, u, width):
    n, tc = mid.shape
    left = w[3:4] * mid
    cent = w[4:5] * mid
    right = w[5:6] * mid
    if up is not None:
        left += w[0:1] * up
        cent += w[1:2] * up
        right += w[2:3] * up
    if down is not None:
        left += w[6:7] * down
        cent += w[7:8] * down
        right += w[8:9] * down
    cidx = lax.broadcasted_iota(jnp.int32, (n, tc), 0) & (width - 1)
    acc = cent + bias
    acc += jnp.where(cidx != 0, pltpu.roll(left, 1, axis=0), 0.0)
    acc += jnp.where(cidx != width - 1, pltpu.roll(right, n - 1, axis=0), 0.0)
    return (_silu(acc) * u).astype(BF16)


def _ffn_mid_body(u_ref, g_ref, w_ref, b_ref, o_ref, *, ctx_rows):
    total = u_ref.shape[0]
    w = w_ref[...]
    bias = b_ref[...]
    o_ref[0:ctx_rows, :] = _conv_rows(None, g_ref[0:ctx_rows, :], None, w, bias, u_ref[0:ctx_rows, :], ctx_rows)
    step = FFN_ROWS
    nsteps = (total - ctx_rows) // step
    zeros = jnp.zeros((GRID_W, u_ref.shape[1]), F32)

    def load(start, n):
        return g_ref[pl.ds(start, n), :]

    def emit(start, up, mid, down):
        o_ref[pl.ds(start, step), :] = _conv_rows(up, mid, down, w, bias, u_ref[pl.ds(start, step), :], GRID_W)

    first = ctx_rows
    last = ctx_rows + (nsteps - 1) * step
    if nsteps == 1:
        mid = load(first, step)
        emit(first, jnp.concatenate([zeros, mid[:step - GRID_W]], axis=0), mid,
             jnp.concatenate([mid[GRID_W:], zeros], axis=0))
        return
    mid = load(first, step)
    emit(first, jnp.concatenate([zeros, mid[:step - GRID_W]], axis=0), mid, load(first + GRID_W, step))

    def body(s, carry):
        start = pl.multiple_of(ctx_rows + s * step, GRID_W)
        emit(start, load(start - GRID_W, step), load(start, step), load(start + GRID_W, step))
        return carry

    lax.fori_loop(1, nsteps - 1, body, 0)
    mid = load(last, step)
    emit(last, load(last - GRID_W, step), mid, jnp.concatenate([mid[GRID_W:], zeros], axis=0))


def _ffn_mid(up, conv_w9, conv_b, nb, ctx_rows, tc=LANE):
    m, two_f = up.shape
    f = two_f // 2
    total = m // nb
    nct = f // tc
    return pl.pallas_call(
        functools.partial(_ffn_mid_body, ctx_rows=ctx_rows),
        grid=(nb, nct),
        in_specs=[
            pl.BlockSpec((total, tc), lambda b, j: (b, j)),
            pl.BlockSpec((total, tc), lambda b, j: (b, nct + j)),
            pl.BlockSpec((9, tc), lambda b, j: (0, j)),
            pl.BlockSpec((1, tc), lambda b, j: (0, j)),
        ],
        out_specs=pl.BlockSpec((total, tc), lambda b, j: (b, j)),
        out_shape=jax.ShapeDtypeStruct((m, f), BF16),
        compiler_params=_cp("parallel", "parallel"),
        name="ffn_conv_gate",
    )(up, up, conv_w9, conv_b)


def _final_norm_body(x_ref, g_ref, o_ref):
    x = x_ref[...]
    o_ref[...] = x * lax.rsqrt(jnp.mean(x * x, axis=-1, keepdims=True) + EPS) * g_ref[...]


def _final_norm(x, g, nb, nblk, n_ctx_blk, seq):
    d = x.shape[1]
    tr = TOKEN_BLOCK
    return pl.pallas_call(
        _final_norm_body,
        grid=(nb, seq // tr),
        in_specs=[
            pl.BlockSpec((tr, d), lambda b, i: (b * nblk + n_ctx_blk + i, 0)),
            pl.BlockSpec((1, d), lambda b, i: (0, 0)),
        ],
        out_specs=pl.BlockSpec((None, tr, d), lambda b, i: (b, i, 0)),
        out_shape=jax.ShapeDtypeStruct((nb, seq, d), F32),
        compiler_params=_cp("parallel", "parallel"),
        name="final_norm",
    )(x, g.reshape(1, d))


def _lower_bounds(logits):
    p = jax.nn.softmax(logits.astype(F32), axis=0)
    return jnp.clip(jnp.cumsum(p, axis=0) - p[0], 0.0, 1.0)


def _pad_heads(w, heads, width):
    lead = w.shape[:-1]
    wh = w.shape[-1] // heads
    w = w.reshape(lead + (heads, wh))
    w = jnp.pad(w, [(0, 0)] * len(lead) + [(0, 0), (0, width - wh)])
    return w.reshape(lead + (heads * width,))


def _pick_tile(total, candidates):
    for t in candidates:
        if total % t == 0:
            return t
    raise ValueError(f"no tile for {total}")


def kernel(x, c, ctx, c_ctx, w_ada, b_ada, norm1_g, w_in, hgrn_lb_fwd, hgrn_lb_bwd, hgrn_norm_g, lru_conv_w, lru_conv_b, lru_wr_fwd, lru_br_fwd, lru_wi_fwd, lru_bi_fwd, lru_lam_fwd, lru_wr_bwd, lru_br_bwd, lru_wi_bwd, lru_bi_bwd, lru_lam_bwd, lru_norm_g, gla_wa_fwd, gla_ba_fwd, gla_wa_bwd, gla_ba_bwd, gla_norm_g, w_out, norm2_g, w_up, ffn_conv_w, ffn_conv_b, w_down, final_norm_g):
    nb, seq, d = x.shape
    ctx_len = ctx.shape[1]
    depth = w_ada.shape[0]
    assert ctx_len == TOKEN_BLOCK and seq % FFN_ROWS == 0
    rows_per_batch = ctx_len + seq
    nblk = rows_per_batch // TOKEN_BLOCK
    n_ctx_blk = ctx_len // TOKEN_BLOCK
    m = nb * rows_per_batch

    a_w = hgrn_norm_g.shape[1]
    b_w = lru_norm_g.shape[1]
    c_w = gla_norm_g.shape[1]
    c_k = gla_ba_fwd.shape[1]
    c_dk = c_k // C_HEADS
    c_kp = -(-c_dk // LANE) * LANE
    n_blocks = b_w // B_BW

    ta, tb_, tck, tcv = a_w // LANE, b_w // LANE, C_HEADS * c_kp // LANE, c_w // LANE
    off_bx = 5 * ta
    off_ab = off_bx + tb_
    off_bg = -(-(off_ab + 1) // tb_) * tb_
    off_cq = -(-(off_bg + tb_) // tck) * tck
    off_ck = off_cq + tck
    off_cv = -(-(off_ck + tck) // tcv) * tcv
    off_cg = off_cv + tcv
    n_proj = (off_cg + tcv) * LANE
    tn_in = _pick_tile(n_proj, (1024, 512, 256, 128))
    n_proj = -(-n_proj // tn_in) * tn_in

    tm = _pick_tile(rows_per_batch, (768, 512, 256))

    xs = jnp.concatenate([ctx, x], axis=1).reshape(m, d)

    c16 = jnp.zeros((16, d), F32).at[:nb].set(c).at[nb].set(c_ctx)
    mods = _ada(c16, w_ada, b_ada)
    lbf_all = _lower_bounds(hgrn_lb_fwd)
    lbb_all = _lower_bounds(hgrn_lb_bwd)

    def place(cols, parts):
        out, pos = [], 0
        for off, arr in sorted(parts, key=lambda p: p[0]):
            start = off * LANE
            if start > pos:
                out.append(jnp.zeros((d, start - pos), arr.dtype))
            out.append(arr)
            pos = start + arr.shape[1]
        if pos < cols:
            out.append(jnp.zeros((d, cols - pos), parts[0][1].dtype))
        return jnp.concatenate(out, axis=1)

    for l in range(depth):
        mod = mods[l, :nb + 1].reshape(nb + 1, 6, 1, d)
        sh1, sc1, g1, sh2, sc2, g2 = (mod[:, i] for i in range(6))

        wl = w_in[l]
        o = 0
        sec = {}
        for name, size in (("a", 5 * a_w), ("bx", b_w), ("bg", b_w), ("cq", c_k), ("ck", c_k), ("cv", c_w),
                           ("cg", c_w), ("ab", 2 * C_RANK)):
            sec[name] = wl[:, o:o + size].astype(BF16)
            o += size
        w_proj = place(n_proj, [
            (0, sec["a"]), (off_bx, sec["bx"]), (off_ab, sec["ab"]), (off_bg, sec["bg"]),
            (off_cq, _pad_heads(sec["cq"], C_HEADS, c_kp)), (off_ck, _pad_heads(sec["ck"], C_HEADS, c_kp)),
            (off_cv, sec["cv"]), (off_cg, sec["cg"]),
        ])

        h = _norm_mod(xs, norm1_g[l], sh1, sc1, nblk, n_ctx_blk)
        proj = _matmul(h, w_proj, tm, tn_in)

        lbf = lbf_all[l].reshape(1, a_w)
        lbb = lbb_all[l].reshape(1, a_w)
        oa = _hgrn(proj, lbf, nb, nblk, reverse=False)
        ya = _hgrn(proj, lbb, nb, nblk, reverse=True, o_fwd=oa, gain=hgrn_norm_g[l].reshape(1, a_w))

        w_gates = jnp.concatenate([lru_wr_fwd[l], lru_wi_fwd[l], lru_wr_bwd[l], lru_wi_bwd[l]], axis=-1).astype(BF16)
        b_gates = jnp.concatenate([t[l].reshape(n_blocks, 1, B_BW) for t in
                                   (lru_br_fwd, lru_bi_fwd, lru_br_bwd, lru_bi_bwd)], axis=-1)
        lam = jnp.stack([lru_lam_fwd[l], lru_lam_bwd[l]], axis=0)
        hb = _lru(proj, off_bx, lru_conv_w[l], lru_conv_b[l].reshape(1, b_w), w_gates, b_gates, lam,
                  nb, nblk, n_ctx_blk)
        yb = _lru_fin(hb, proj, off_bg, lru_norm_g[l].reshape(1, b_w))

        cols = (off_cq, off_ck, off_cv, off_cg, off_ab)
        kw = C_HEADS * c_kp

        def gate_params(wa, ba, row0):
            wp = jnp.zeros((LANE, kw), F32).at[row0:row0 + C_RANK].set(_pad_heads(wa, C_HEADS, c_kp))
            hi = wp.astype(BF16)
            lo = (wp - hi.astype(F32)).astype(BF16)
            return jnp.concatenate([hi, hi, lo], axis=0), _pad_heads(ba, C_HEADS, c_kp).reshape(1, kw)

        waf, baf = gate_params(gla_wa_fwd[l], gla_ba_fwd[l], 0)
        wab, bab = gate_params(gla_wa_bwd[l], gla_ba_bwd[l], C_RANK)
        q_scale = c_dk ** -0.5
        gain_c = gla_norm_g[l].reshape(1, c_w)
        oc = _glac(proj, cols, waf, baf, q_scale, c_w, nb, nblk, reverse=False)
        yc = _glac(proj, cols, wab, bab, q_scale, c_w, nb, nblk, reverse=True, o_fwd=oc, gain=gain_c)

        wo = w_out[l]
        wo_parts = [wo[:a_w].astype(BF16), wo[a_w:a_w + b_w].astype(BF16), wo[a_w + b_w:].astype(BF16)]
        xs = _matmul_res([ya, yb, yc], wo_parts, xs, g1, tm, 1024, rows_per_batch, ctx_len)

        h2 = _norm_mod(xs, norm2_g[l], sh2, sc2, nblk, n_ctx_blk)
        up = _matmul(h2, w_up[l].astype(BF16), tm, 1024)
        act = _ffn_mid(up, ffn_conv_w[l].reshape(9, -1), ffn_conv_b[l].reshape(1, -1), nb, ctx_len)
        xs = _matmul_res([act], [w_down[l].astype(BF16)], xs, g2, tm, 1024, rows_per_batch, ctx_len)

    return _final_norm(xs, final_norm_g, nb, nblk, n_ctx_blk, seq)
```

```python
import functools

import jax
import jax.numpy as jnp
from jax import lax
from jax.experimental import pallas as pl
from jax.experimental.pallas import tpu as pltpu

F32 = jnp.float32
BF16 = jnp.bfloat16
HIGHEST = lax.Precision.HIGHEST

EPS = 1e-6
GRID_W = 64
LANE = 128
SUBLANE = 8
A_DK = 128
A_HEADS_PER_STEP = 12
B_BW = 128
B_CONV = 4
RG_C = 8.0
C_HEADS = 4
C_RANK = 16
GLA_NORMALIZER = 16.0
TOKEN_BLOCK = 256
GLA_CHUNK = 128
GLA_SUB = 32
LOG2 = 0.6931471805599453
TINY = 1e-30
BOUND_MARGIN = 1.02
SAFE_HALF_DECAY = 60.0
FFN_ROWS = 512
VMEM_LIMIT = 56 * 1024 * 1024


def _cp(*sem):
    return pltpu.CompilerParams(dimension_semantics=sem, vmem_limit_bytes=VMEM_LIMIT)


def _dot_nt(a, b):
    return lax.dot_general(a, b, (((1,), (1,)), ((), ())), preferred_element_type=F32)


def _dot_tn(a, b):
    return lax.dot_general(a, b, (((0,), (0,)), ((), ())), preferred_element_type=F32)


def _silu(v):
    return v * jax.nn.sigmoid(v)


def _ada_body(c_ref, w_ref, b_ref, o_ref):
    a = _silu(c_ref[...]).astype(BF16)
    o_ref[...] = jnp.dot(a, w_ref[...].astype(BF16), preferred_element_type=F32) + b_ref[...]


def _ada(c16, w_ada, b_ada, tn=512):
    depth, d, n = w_ada.shape
    rows = c16.shape[0]
    return pl.pallas_call(
        _ada_body,
        grid=(depth, n // tn),
        in_specs=[
            pl.BlockSpec((rows, d), lambda l, j: (0, 0)),
            pl.BlockSpec((None, d, tn), lambda l, j: (l, 0, j)),
            pl.BlockSpec((None, 1, tn), lambda l, j: (l, 0, j)),
        ],
        out_specs=pl.BlockSpec((None, rows, tn), lambda l, j: (l, 0, j)),
        out_shape=jax.ShapeDtypeStruct((depth, rows, n), F32),
        compiler_params=_cp("parallel", "parallel"),
        name="ada_mod",
    )(c16, w_ada, b_ada.reshape(depth, 1, n))


def _norm_mod_body(x_ref, g_ref, sh_ref, sc_ref, o_ref):
    x = x_ref[...]
    y = x * lax.rsqrt(jnp.mean(x * x, axis=-1, keepdims=True) + EPS) * g_ref[...]
    o_ref[...] = (y * (1.0 + sc_ref[...]) + sh_ref[...]).astype(o_ref.dtype)


def _norm_mod(x, g, shift, scale, nblk, n_ctx_blk):
    m, d = x.shape
    tr = TOKEN_BLOCK
    nb = shift.shape[0] - 1

    def mod_row(i):
        return jnp.where(i % nblk < n_ctx_blk, nb, i // nblk)

    return pl.pallas_call(
        _norm_mod_body,
        grid=(m // tr,),
        in_specs=[
            pl.BlockSpec((tr, d), lambda i: (i, 0)),
            pl.BlockSpec((1, d), lambda i: (0, 0)),
            pl.BlockSpec((None, 1, d), lambda i: (mod_row(i), 0, 0)),
            pl.BlockSpec((None, 1, d), lambda i: (mod_row(i), 0, 0)),
        ],
        out_specs=pl.BlockSpec((tr, d), lambda i: (i, 0)),
        out_shape=jax.ShapeDtypeStruct((m, d), BF16),
        compiler_params=_cp("parallel"),
        name="norm_mod",
    )(x, g.reshape(1, d), shift, scale)


def _mm_body(a_ref, b_ref, o_ref):
    o_ref[...] = jnp.dot(a_ref[...], b_ref[...], preferred_element_type=F32).astype(o_ref.dtype)


def _matmul(a, b, tm, tn, out_dtype=F32, a_resident=False):
    m, k = a.shape
    n = b.shape[1]
    if a_resident:
        grid = (m // tm, n // tn)
        a_map, b_map, o_map = (lambda i, j: (i, 0)), (lambda i, j: (0, j)), (lambda i, j: (i, j))
    else:
        grid = (n // tn, m // tm)
        a_map, b_map, o_map = (lambda j, i: (i, 0)), (lambda j, i: (0, j)), (lambda j, i: (i, j))
    return pl.pallas_call(
        _mm_body,
        grid=grid,
        in_specs=[pl.BlockSpec((tm, k), a_map), pl.BlockSpec((k, tn), b_map)],
        out_specs=pl.BlockSpec((tm, tn), o_map),
        out_shape=jax.ShapeDtypeStruct((m, n), out_dtype),
        compiler_params=_cp("parallel", "parallel"),
        name="proj_matmul",
    )(a, b)


def _mm_res_body(*refs, n_ops, ctx_rows, tiles_per_batch):
    a_refs = refs[:n_ops]
    b_refs = refs[n_ops:2 * n_ops]
    r_ref, gb_ref, gc_ref, o_ref = refs[2 * n_ops:]
    acc = jnp.dot(a_refs[0][...], b_refs[0][...], preferred_element_type=F32)
    for a_ref, b_ref in zip(a_refs[1:], b_refs[1:]):
        acc += jnp.dot(a_ref[...], b_ref[...], preferred_element_type=F32)
    tm = acc.shape[0]
    first = pl.program_id(1) % tiles_per_batch == 0
    row = lax.broadcasted_iota(jnp.int32, (tm, 1), 0)
    is_ctx = jnp.logical_and(first, row < ctx_rows)
    gate = jnp.where(is_ctx, gc_ref[...], gb_ref[...])
    o_ref[...] = r_ref[...] + gate * acc


def _matmul_res(a_list, b_list, resid, gate, tm, tn, rows_per_batch, ctx_rows):
    m, n = resid.shape
    nb = gate.shape[0] - 1
    tiles_per_batch = rows_per_batch // tm
    assert ctx_rows <= tm
    n_ops = len(a_list)
    in_specs = [pl.BlockSpec((tm, a.shape[1]), lambda j, i: (i, 0)) for a in a_list]
    in_specs += [pl.BlockSpec((b.shape[0], tn), lambda j, i: (0, j)) for b in b_list]
    in_specs += [
        pl.BlockSpec((tm, tn), lambda j, i: (i, j)),
        pl.BlockSpec((None, 1, tn), lambda j, i: (i // tiles_per_batch, 0, j)),
        pl.BlockSpec((None, 1, tn), lambda j, i: (nb, 0, j)),
    ]
    return pl.pallas_call(
        functools.partial(_mm_res_body, n_ops=n_ops, ctx_rows=ctx_rows, tiles_per_batch=tiles_per_batch),
        grid=(n // tn, m // tm),
        in_specs=in_specs,
        out_specs=pl.BlockSpec((tm, tn), lambda j, i: (i, j)),
        out_shape=jax.ShapeDtypeStruct((m, n), F32),
        compiler_params=_cp("parallel", "parallel"),
        name="res_matmul",
    )(*a_list, *b_list, resid, gate, gate)


def _split3(g):
    hi = g.astype(BF16)
    r = g - hi.astype(F32)
    mid = r.astype(BF16)
    lo = (r - mid.astype(F32)).astype(BF16)
    return jnp.concatenate([hi, mid, lo], axis=1)


def _blocks(rows_list):
    return jnp.concatenate([jnp.broadcast_to(r, (GLA_SUB, r.shape[1])) for r in rows_list], axis=0)


def _window_decay_may_be_large(lower_bound_rows, nrows):
    worst = None
    for r0 in range(0, nrows, GLA_SUB):
        part = None
        for r in range(r0, r0 + GLA_SUB, SUBLANE):
            rows = lower_bound_rows(r)
            part = rows if part is None else part + rows
        worst = part if worst is None else jnp.minimum(worst, part)
    return SUBLANE * jnp.min(worst) < -2.0 * SAFE_HALF_DECAY


def _diag_scores_safe(q, k, g, reverse):
    c = q.shape[0]
    kdim = q.shape[1]
    row = lax.broadcasted_iota(jnp.int32, (c, c), 0)
    col = lax.broadcasted_iota(jnp.int32, (c, c), 1)
    prow = lax.broadcasted_iota(jnp.int32, (c, kdim), 0)
    g3 = _split3(g)
    kb = k.astype(BF16)
    scores = jnp.where(row == col, _dot_nt(q.astype(BF16), kb), 0.0)
    w = 1
    while w < GLA_SUB:
        inblock = 2 * w - 1
        mid = (row & ~inblock) + w
        if reverse:
            take = ((col >= row) & (col < mid)) | ((row >= mid) & (col >= mid) & (col < row))
            is_query = (prow & inblock) < w
        else:
            take = ((col >= mid) & (col <= row)) | ((row < mid) & (col > row) & (col < mid))
            is_query = (prow & inblock) >= w
        ee = jnp.dot(jnp.where(take, 1.0, 0.0).astype(BF16), g3, preferred_element_type=F32)
        factor = jnp.exp(ee[:, :kdim] + ee[:, kdim:2 * kdim] + ee[:, 2 * kdim:])
        qm = jnp.where(is_query, q * factor, 0.0).astype(BF16)
        km = jnp.where(is_query, 0.0, k * factor).astype(BF16)
        same = (row & ~inblock) == (col & ~inblock)
        scores = scores + jnp.where(same, _dot_nt(qm, km), 0.0)
        w *= 2
    return scores


def _gla_chunks(qs, ks, vs, gs, sts, reverse, robust, safe_diags):
    c = qs[0].shape[0]
    nsub = c // GLA_SUB
    row = lax.broadcasted_iota(jnp.int32, (c, c), 0)
    col = lax.broadcasted_iota(jnp.int32, (c, c), 1)
    tri = jnp.where((col >= row) if reverse else (col <= row), 1.0, 0.0).astype(BF16)
    srow = lax.broadcasted_iota(jnp.int32, (GLA_SUB, c), 0)
    scol = lax.broadcasted_iota(jnp.int32, (GLA_SUB, c), 1)
    order = list(range(nsub - 1, -1, -1)) if reverse else list(range(nsub))

    bs = []
    for g in gs:
        kdim = g.shape[1]
        bb = jnp.dot(tri, _split3(g), preferred_element_type=F32)
        bs.append(bb[:, :kdim] + bb[:, kdim:2 * kdim] + bb[:, 2 * kdim:])

    work = []
    for q, k, b in zip(qs, ks, bs):
        zero = jnp.zeros((1, q.shape[1]), F32)
        entry, leave = [None] * nsub, [None] * nsub
        for i in range(nsub):
            r0, r1 = i * GLA_SUB, (i + 1) * GLA_SUB
            if reverse:
                entry[i] = b[r1:r1 + 1] if r1 < c else zero
                leave[i] = b[r0:r0 + 1]
            else:
                entry[i] = b[r0 - 1:r0] if r0 > 0 else zero
                leave[i] = b[r1 - 1:r1]
        edge = leave[order[-1]]
        qi = q * jnp.exp(b - _blocks(entry))
        kh = k * jnp.exp(_blocks(leave) - b)
        half = _blocks([jnp.exp(0.5 * (entry[i] - leave[i])) for i in range(nsub)])
        q_in = (qi * _blocks([jnp.exp(e) for e in entry])).astype(BF16)
        k_out = (kh * _blocks([jnp.exp(edge - x) for x in leave])).astype(BF16)
        qd = (qi * half).astype(BF16)
        kd = (kh * half).astype(BF16)
        pairs = []
        for pos, i in enumerate(order):
            for j in order[:pos]:
                r0 = i * GLA_SUB
                pairs.append((i, j, (qi[r0:r0 + GLA_SUB] * jnp.exp(entry[i] - leave[j])).astype(BF16)))
        work.append((q_in, k_out, qd, kd, kh.astype(BF16), pairs, edge))

    atts = []
    for (q_in, k_out, qd, kd, khb, pairs, edge), safe_diag in zip(work, safe_diags):
        diag = jnp.where(robust, safe_diag, _dot_nt(qd, kd))
        off = _dot_nt(jnp.concatenate([p[2] for p in pairs], axis=0), khb)
        blocks = []
        for i in range(nsub):
            r0 = i * GLA_SUB
            inside = (scol >= r0) & (scol < r0 + GLA_SUB)
            keep = inside & ((scol - r0 >= srow) if reverse else (scol - r0 <= srow))
            att = jnp.where(keep, diag[r0:r0 + GLA_SUB], 0.0)
            for n, (pi, pj, _) in enumerate(pairs):
                if pi == i:
                    c0 = pj * GLA_SUB
                    att = jnp.where((scol >= c0) & (scol < c0 + GLA_SUB), off[n * GLA_SUB:(n + 1) * GLA_SUB], att)
            blocks.append(att.astype(BF16))
        atts.append(jnp.concatenate(blocks, axis=0))

    outs, new_sts = [], []
    for (q_in, k_out, qd, kd, khb, pairs, edge), att, v, st in zip(work, atts, vs, sts):
        vb = v.astype(BF16)
        outs.append(jnp.dot(att, vb, preferred_element_type=F32) + _dot_nt(q_in, st.astype(BF16)))
        new_sts.append(st * jnp.exp(edge) + _dot_tn(vb, k_out))
    return outs, new_sts


def _seq_block(t, nblk, reverse):
    if not reverse:
        return t
    n_ctx = 1
    return jnp.where(t < n_ctx, n_ctx - 1 - t, nblk - 1 - (t - n_ctx))


def _hgrn_body(q_ref, z_ref, v_ref, lb_ref, *rest, reverse, finish):
    if finish:
        of_ref, g_ref, gain_ref, y_ref, st_ref, diag_ref = rest
    else:
        o_ref, st_ref, diag_ref = rest

    @pl.when(pl.program_id(2) == 0)
    def _():
        st_ref[...] = jnp.zeros_like(st_ref)
        diag_ref[...] = jnp.zeros_like(diag_ref)

    tb = q_ref.shape[0]
    heads = q_ref.shape[1] // A_DK
    nchunk = tb // GLA_CHUNK
    order = range(nchunk - 1, -1, -1) if reverse else range(nchunk)
    lb = lb_ref[...]
    cols = [slice(h * A_DK, (h + 1) * A_DK) for h in range(heads)]

    def prep(rs):
        qh = _silu(q_ref[rs, :]) * (A_DK ** -0.5)
        sig = jax.nn.sigmoid(z_ref[rs, :])
        key = (1.0 - lb) * (1.0 - sig)
        logf = jnp.log(lb + (1.0 - lb) * sig)
        return qh, key, logf

    robust = _window_decay_may_be_large(
        lambda r: jnp.minimum(z_ref[r:r + SUBLANE, :], 0.0) - LOG2, tb)

    @pl.when(robust)
    def _():
        for c in order:
            rs = slice(c * GLA_CHUNK, (c + 1) * GLA_CHUNK)
            qh, key, logf = prep(rs)
            for h, cs in enumerate(cols):
                diag_ref[c * heads + h] = _diag_scores_safe(qh[:, cs], key[:, cs], logf[:, cs], reverse)

    sts = [st_ref[h] for h in range(heads)]
    for c in order:
        rs = slice(c * GLA_CHUNK, (c + 1) * GLA_CHUNK)
        qh, key, logf = prep(rs)
        v = v_ref[rs, :]
        outs, sts = _gla_chunks([qh[:, cs] for cs in cols], [key[:, cs] for cs in cols], [v[:, cs] for cs in cols],
                                [logf[:, cs] for cs in cols], sts, reverse, robust,
                                [diag_ref[c * heads + h] for h in range(heads)])
        for cs, o in zip(cols, outs):
            if finish:
                o = o + of_ref[rs, cs]
                y = o * lax.rsqrt(jnp.mean(o * o, axis=-1, keepdims=True) + EPS)
                y_ref[rs, cs] = (y * gain_ref[:, cs] * _silu(g_ref[rs, cs])).astype(y_ref.dtype)
            else:
                o_ref[rs, cs] = o
    for h in range(heads):
        st_ref[h] = sts[h]


def _hgrn(proj, lb, nb, nblk, reverse, o_fwd=None, gain=None):
    m = proj.shape[0]
    width = lb.shape[1]
    hw = A_HEADS_PER_STEP * A_DK
    nhb = width // hw
    tb = TOKEN_BLOCK
    finish = o_fwd is not None

    def rows(b, hb, t):
        return b * nblk + _seq_block(t, nblk, reverse)

    def spec(section):
        return pl.BlockSpec((tb, hw), lambda b, hb, t: (rows(b, hb, t), section * nhb + hb))

    in_specs = [spec(0), spec(2 if reverse else 1), spec(3), pl.BlockSpec((1, hw), lambda b, hb, t: (0, hb))]
    args = [proj, proj, proj, lb]
    if finish:
        in_specs += [pl.BlockSpec((tb, hw), lambda b, hb, t: (rows(b, hb, t), hb)), spec(4),
                     pl.BlockSpec((1, hw), lambda b, hb, t: (0, hb))]
        args += [o_fwd, proj, gain]
        out_dtype = BF16
    else:
        out_dtype = F32
    return pl.pallas_call(
        functools.partial(_hgrn_body, reverse=reverse, finish=finish),
        grid=(nb, nhb, nblk),
        in_specs=in_specs,
        out_specs=pl.BlockSpec((tb, hw), lambda b, hb, t: (rows(b, hb, t), hb)),
        out_shape=jax.ShapeDtypeStruct((m, width), out_dtype),
        scratch_shapes=[pltpu.VMEM((A_HEADS_PER_STEP, A_DK, A_DK), F32),
                        pltpu.VMEM((A_HEADS_PER_STEP * (tb // GLA_CHUNK), GLA_CHUNK, GLA_CHUNK), F32)],
        compiler_params=_cp("parallel", "parallel", "arbitrary"),
        name="hgrn_bwd" if reverse else "hgrn_fwd",
    )(*args)


def _log_sigmoid(v):
    return jnp.minimum(v, 0.0) - jnp.log(1.0 + jnp.exp(-jnp.abs(v)))


def _glac_body(q_ref, k_ref, v_ref, a_ref, wa_ref, ba_ref, wb_ref, fb_ref, *rest, reverse, finish, q_scale):
    if finish:
        of_ref, g_ref, gain_ref, y_ref, st_ref, diag_ref = rest
    else:
        o_ref, st_ref, diag_ref = rest

    @pl.when(pl.program_id(1) == 0)
    def _():
        st_ref[...] = jnp.zeros_like(st_ref)
        diag_ref[...] = jnp.zeros_like(diag_ref)

    tb = q_ref.shape[0]
    kp = q_ref.shape[1] // C_HEADS
    vw = v_ref.shape[1] // C_HEADS
    nchunk = tb // GLA_CHUNK
    order = range(nchunk - 1, -1, -1) if reverse else range(nchunk)
    a = a_ref[...]
    reach = jnp.dot(jnp.abs(a).astype(BF16), wb_ref[...], preferred_element_type=F32) + fb_ref[...]
    floor = reach * (-1.0 / GLA_NORMALIZER)
    robust = _window_decay_may_be_large(lambda r: floor[r:r + SUBLANE], tb)
    a_hi = a.astype(BF16)
    a_lo = (a - a_hi.astype(F32)).astype(BF16)
    logits = jnp.dot(jnp.concatenate([a_hi, a_lo, a_hi], axis=1), wa_ref[...],
                     preferred_element_type=F32) + ba_ref[...]
    kcols = [slice(h * kp, (h + 1) * kp) for h in range(C_HEADS)]
    vcols = [slice(h * vw, (h + 1) * vw) for h in range(C_HEADS)]

    def gate_of(rs):
        return _log_sigmoid(logits[rs]) * (1.0 / GLA_NORMALIZER)

    @pl.when(robust)
    def _():
        for c in order:
            rs = slice(c * GLA_CHUNK, (c + 1) * GLA_CHUNK)
            gate = gate_of(rs)
            for h, ks in enumerate(kcols):
                diag_ref[c * C_HEADS + h] = _diag_scores_safe(q_ref[rs, ks] * q_scale, k_ref[rs, ks], gate[:, ks],
                                                              reverse)

    states = [st_ref[h] for h in range(C_HEADS)]
    for c in order:
        rs = slice(c * GLA_CHUNK, (c + 1) * GLA_CHUNK)
        gate = gate_of(rs)
        outs, states = _gla_chunks([q_ref[rs, ks] * q_scale for ks in kcols], [k_ref[rs, ks] for ks in kcols],
                                   [v_ref[rs, vs] for vs in vcols], [gate[:, ks] for ks in kcols], states, reverse,
                                   robust, [diag_ref[c * C_HEADS + h] for h in range(C_HEADS)])
        for vs, o in zip(vcols, outs):
            if finish:
                o = o + of_ref[rs, vs]
                y = o * lax.rsqrt(jnp.mean(o * o, axis=-1, keepdims=True) + EPS)
                y_ref[rs, vs] = (y * gain_ref[:, vs] * _silu(g_ref[rs, vs])).astype(y_ref.dtype)
            else:
                o_ref[rs, vs] = o
    for h in range(C_HEADS):
        st_ref[h] = states[h]


def _glac(proj, cols, gate_params, q_scale, vw, nb, nblk, reverse, o_fwd=None, gain=None):
    wa_pad, ba_pad, w_bound, bias_bound = gate_params
    m = proj.shape[0]
    kw = wa_pad.shape[1]
    tb = TOKEN_BLOCK
    finish = o_fwd is not None
    q_off, k_off, v_off, g_off, a_off = cols

    def rows(b, t):
        return b * nblk + _seq_block(t, nblk, reverse)

    def spec(width, off):
        assert (off * LANE) % width == 0
        return pl.BlockSpec((tb, width), lambda b, t: (rows(b, t), off * LANE // width))

    in_specs = [spec(kw, q_off), spec(kw, k_off), spec(vw, v_off), spec(LANE, a_off),
                pl.BlockSpec((3 * LANE, kw), lambda b, t: (0, 0)), pl.BlockSpec((1, kw), lambda b, t: (0, 0)),
                pl.BlockSpec((LANE, LANE), lambda b, t: (0, 0)), pl.BlockSpec((1, LANE), lambda b, t: (0, 0))]
    args = [proj, proj, proj, proj, wa_pad, ba_pad, w_bound, bias_bound]
    if finish:
        in_specs += [pl.BlockSpec((tb, vw), lambda b, t: (rows(b, t), 0)), spec(vw, g_off),
                     pl.BlockSpec((1, vw), lambda b, t: (0, 0))]
        args += [o_fwd, proj, gain]
        out_dtype = BF16
    else:
        out_dtype = F32
    return pl.pallas_call(
        functools.partial(_glac_body, reverse=reverse, finish=finish, q_scale=q_scale),
        grid=(nb, nblk),
        in_specs=in_specs,
        out_specs=pl.BlockSpec((tb, vw), lambda b, t: (rows(b, t), 0)),
        out_shape=jax.ShapeDtypeStruct((m, vw), out_dtype),
        scratch_shapes=[pltpu.VMEM((C_HEADS, vw // C_HEADS, kw // C_HEADS), F32),
                        pltpu.VMEM((C_HEADS * (tb // GLA_CHUNK), GLA_CHUNK, GLA_CHUNK), F32)],
        compiler_params=_cp("parallel", "arbitrary"),
        name="gla_bwd" if reverse else "gla_fwd",
    )(*args)


def _lru_body(x_ref, cw_ref, cb_ref, w_ref, bias_ref, lam_ref, o_ref, xc_ref, *, nblk, n_ctx_blk):
    rb = TOKEN_BLOCK
    total = x_ref.shape[0]
    gw = x_ref.shape[1]
    rid = lax.broadcasted_iota(jnp.int32, (SUBLANE, gw), 0)
    cw = cw_ref[...]
    cb = cb_ref[...]

    def conv(blk, start):
        cur = x_ref[pl.ds(start, rb), :]
        seg_start = jnp.logical_or(blk == 0, blk == n_ctx_blk)
        seg_end = jnp.logical_or(blk == n_ctx_blk - 1, blk == nblk - 1)
        pstart = pl.multiple_of(jnp.maximum(start - SUBLANE, 0), SUBLANE)
        nstart = pl.multiple_of(jnp.minimum(start + rb, total - SUBLANE), SUBLANE)
        prev8 = jnp.where(seg_start, 0.0, x_ref[pl.ds(pstart, SUBLANE), :])
        next8 = jnp.where(seg_end, 0.0, x_ref[pl.ds(nstart, SUBLANE), :])
        ext = jnp.concatenate([prev8, cur, next8], axis=0)
        n_ext = rb + 2 * SUBLANE

        def tap(shift):
            return pltpu.roll(ext, shift % n_ext, axis=0)[SUBLANE:SUBLANE + rb]

        return cb + cw[0:1] * tap(2) + cw[1:2] * tap(1) + cw[2:3] * cur + cw[3:4] * tap(-1)

    def run(reverse):
        wsl = slice(2 * gw, 4 * gw) if reverse else slice(0, 2 * gw)
        lam = lam_ref[1:2] if reverse else lam_ref[0:1]
        neg_c_softplus = -RG_C * (jnp.maximum(-lam, 0.0) + jnp.log(1.0 + jnp.exp(-jnp.abs(lam))))
        w = w_ref[:, wsl]
        bias = bias_ref[:, wsl]

        def stage(j):
            blk = _seq_block(j, nblk, reverse)
            start = pl.multiple_of(blk * rb, rb)
            if reverse:
                xc = xc_ref[pl.ds(start, rb), :]
            else:
                xc = conv(blk, start)
                xc_ref[pl.ds(start, rb), :] = xc
            return xc, jnp.dot(xc.astype(BF16), w, preferred_element_type=F32) + bias

        def body(j, carry):
            if reverse:
                h, xc, gates = carry
                xc_next, gates_next = stage(jnp.minimum(j + 1, nblk - 1))
            else:
                h = carry
                xc, gates = stage(j)
            blk = _seq_block(j, nblk, reverse)
            start = pl.multiple_of(blk * rb, rb)
            r = jax.nn.sigmoid(gates[:, :gw])
            i = jax.nn.sigmoid(gates[:, gw:])
            a = jnp.exp(neg_c_softplus * r)
            y = 1.0 - a * a
            u = (y * lax.rsqrt(jnp.maximum(y, TINY))) * (i * xc)
            ngroups = rb // SUBLANE
            order = range(ngroups - 1, -1, -1) if reverse else range(ngroups)
            last = slice(0, 1) if reverse else slice(SUBLANE - 1, SUBLANE)
            scanned = [None] * ngroups
            for gi in order:
                gs = slice(gi * SUBLANE, (gi + 1) * SUBLANE)
                ag, ug = a[gs], u[gs]
                for d in (1, 2, 4):
                    keep = (rid < SUBLANE - d) if reverse else (rid >= d)
                    shift = SUBLANE - d if reverse else d
                    a_s = jnp.where(keep, pltpu.roll(ag, shift, axis=0), 1.0)
                    u_s = jnp.where(keep, pltpu.roll(ug, shift, axis=0), 0.0)
                    ug = ag * u_s + ug
                    ag = ag * a_s
                scanned[gi] = (ag, ug)
            carry = [None] * ngroups
            for gi in order:
                ag, ug = scanned[gi]
                carry[gi] = h
                h = ag[last] * h + ug[last]
            hblk = jnp.concatenate([scanned[gi][1] + scanned[gi][0] * carry[gi] for gi in range(ngroups)], axis=0)
            if reverse:
                o_ref[pl.ds(start, rb), :] = o_ref[pl.ds(start, rb), :] + hblk
            else:
                o_ref[pl.ds(start, rb), :] = hblk
            return (h, xc_next, gates_next) if reverse else h

        h0 = jnp.zeros((1, gw), F32)
        lax.fori_loop(0, nblk, body, (h0,) + stage(0) if reverse else h0)

    run(False)
    run(True)


def _lru(proj, x_off, conv_w, conv_b, w_gates, b_gates, lam, nb, nblk, n_ctx_blk):
    m = proj.shape[0]
    nblocks = w_gates.shape[0]
    total = m // nb
    gw = B_BW
    return pl.pallas_call(
        functools.partial(_lru_body, nblk=nblk, n_ctx_blk=n_ctx_blk),
        grid=(nb, nblocks),
        in_specs=[
            pl.BlockSpec((total, gw), lambda b, n: (b, x_off + n)),
            pl.BlockSpec((B_CONV, gw), lambda b, n: (0, n)),
            pl.BlockSpec((1, gw), lambda b, n: (0, n)),
            pl.BlockSpec((None, gw, 4 * gw), lambda b, n: (n, 0, 0)),
            pl.BlockSpec((None, 1, 4 * gw), lambda b, n: (n, 0, 0)),
            pl.BlockSpec((2, gw), lambda b, n: (0, n)),
        ],
        out_specs=pl.BlockSpec((total, gw), lambda b, n: (b, n)),
        out_shape=jax.ShapeDtypeStruct((m, nblocks * gw), F32),
        scratch_shapes=[pltpu.VMEM((total, gw), F32)],
        compiler_params=_cp("parallel", "parallel"),
        name="rglru_scan",
    )(proj, conv_w, conv_b, w_gates, b_gates, lam)


def _gelu_tanh(v):
    return 0.5 * v * (1.0 + jnp.tanh(0.7978845608028654 * (v + 0.044715 * (v * v * v))))


def _lru_fin_body(h_ref, gate_ref, gain_ref, y_ref):
    h = h_ref[...]
    y = h * lax.rsqrt(jnp.mean(h * h, axis=-1, keepdims=True) + EPS) * gain_ref[...]
    y_ref[...] = (y * _gelu_tanh(gate_ref[...])).astype(y_ref.dtype)


def _lru_fin(h, proj, gate_off, gain):
    m, w = h.shape
    tr = TOKEN_BLOCK
    assert (gate_off * LANE) % w == 0
    return pl.pallas_call(
        _lru_fin_body,
        grid=(m // tr,),
        in_specs=[
            pl.BlockSpec((tr, w), lambda i: (i, 0)),
            pl.BlockSpec((tr, w), lambda i: (i, gate_off * LANE // w)),
            pl.BlockSpec((1, w), lambda i: (0, 0)),
        ],
        out_specs=pl.BlockSpec((tr, w), lambda i: (i, 0)),
        out_shape=jax.ShapeDtypeStruct((m, w), BF16),
        compiler_params=_cp("parallel"),
        name="rglru_finish",
    )(h, proj, gain)


def _conv_rows(up, mid, down, w, bias, u, width):
    n, tc = mid.shape
    left = w[3:4] * mid
    cent = w[4:5] * mid
    right = w[5:6] * mid
    if up is not None:
        left += w[0:1] * up
        cent += w[1:2] * up
        right += w[2:3] * up
    if down is not None:
        left += w[6:7] * down
        cent += w[7:8] * down
        right += w[8:9] * down
    cidx = lax.broadcasted_iota(jnp.int32, (n, tc), 0) & (width - 1)
    acc = cent + bias
    acc += jnp.where(cidx != 0, pltpu.roll(left, 1, axis=0), 0.0)
    acc += jnp.where(cidx != width - 1, pltpu.roll(right, n - 1, axis=0), 0.0)
    return (_silu(acc) * u).astype(BF16)


def _ffn_mid_body(u_ref, g_ref, w_ref, b_ref, o_ref, *, ctx_rows):
    total = u_ref.shape[0]
    w = w_ref[...]
    bias = b_ref[...]
    o_ref[0:ctx_rows, :] = _conv_rows(None, g_ref[0:ctx_rows, :], None, w, bias, u_ref[0:ctx_rows, :], ctx_rows)
    step = FFN_ROWS
    nsteps = (total - ctx_rows) // step
    zeros = jnp.zeros((GRID_W, u_ref.shape[1]), F32)

    def load(start, n):
        return g_ref[pl.ds(start, n), :]

    def emit(start, up, mid, down):
        o_ref[pl.ds(start, step), :] = _conv_rows(up, mid, down, w, bias, u_ref[pl.ds(start, step), :], GRID_W)

    first = ctx_rows
    last = ctx_rows + (nsteps - 1) * step
    if nsteps == 1:
        mid = load(first, step)
        emit(first, jnp.concatenate([zeros, mid[:step - GRID_W]], axis=0), mid,
             jnp.concatenate([mid[GRID_W:], zeros], axis=0))
        return
    mid = load(first, step)
    emit(first, jnp.concatenate([zeros, mid[:step - GRID_W]], axis=0), mid, load(first + GRID_W, step))

    def body(s, carry):
        start = pl.multiple_of(ctx_rows + s * step, GRID_W)
        emit(start, load(start - GRID_W, step), load(start, step), load(start + GRID_W, step))
        return carry

    lax.fori_loop(1, nsteps - 1, body, 0)
    mid = load(last, step)
    emit(last, load(last - GRID_W, step), mid, jnp.concatenate([mid[GRID_W:], zeros], axis=0))


def _ffn_mid(up, conv_w9, conv_b, nb, ctx_rows, tc=LANE):
    m, two_f = up.shape
    f = two_f // 2
    total = m // nb
    nct = f // tc
    return pl.pallas_call(
        functools.partial(_ffn_mid_body, ctx_rows=ctx_rows),
        grid=(nb, nct),
        in_specs=[
            pl.BlockSpec((total, tc), lambda b, j: (b, j)),
            pl.BlockSpec((total, tc), lambda b, j: (b, nct + j)),
            pl.BlockSpec((9, tc), lambda b, j: (0, j)),
            pl.BlockSpec((1, tc), lambda b, j: (0, j)),
        ],
        out_specs=pl.BlockSpec((total, tc), lambda b, j: (b, j)),
        out_shape=jax.ShapeDtypeStruct((m, f), BF16),
        compiler_params=_cp("parallel", "parallel"),
        name="ffn_conv_gate",
    )(up, up, conv_w9, conv_b)


def _final_norm_body(x_ref, g_ref, o_ref):
    x = x_ref[...]
    o_ref[...] = x * lax.rsqrt(jnp.mean(x * x, axis=-1, keepdims=True) + EPS) * g_ref[...]


def _final_norm(x, g, nb, nblk, n_ctx_blk, seq):
    d = x.shape[1]
    tr = TOKEN_BLOCK
    return pl.pallas_call(
        _final_norm_body,
        grid=(nb, seq // tr),
        in_specs=[
            pl.BlockSpec((tr, d), lambda b, i: (b * nblk + n_ctx_blk + i, 0)),
            pl.BlockSpec((1, d), lambda b, i: (0, 0)),
        ],
        out_specs=pl.BlockSpec((None, tr, d), lambda b, i: (b, i, 0)),
        out_shape=jax.ShapeDtypeStruct((nb, seq, d), F32),
        compiler_params=_cp("parallel", "parallel"),
        name="final_norm",
    )(x, g.reshape(1, d))


def _lower_bounds(logits):
    p = jax.nn.softmax(logits.astype(F32), axis=0)
    return jnp.clip(jnp.cumsum(p, axis=0) - p[0], 0.0, 1.0)


def _pad_heads(w, heads, width):
    lead = w.shape[:-1]
    wh = w.shape[-1] // heads
    w = w.reshape(lead + (heads, wh))
    w = jnp.pad(w, [(0, 0)] * len(lead) + [(0, 0), (0, width - wh)])
    return w.reshape(lead + (heads * width,))


def _pick_tile(total, candidates):
    for t in candidates:
        if total % t == 0:
            return t
    raise ValueError(f"no tile for {total}")


def kernel(x, c, ctx, c_ctx, w_ada, b_ada, norm1_g, w_in, hgrn_lb_fwd, hgrn_lb_bwd, hgrn_norm_g, lru_conv_w, lru_conv_b, lru_wr_fwd, lru_br_fwd, lru_wi_fwd, lru_bi_fwd, lru_lam_fwd, lru_wr_bwd, lru_br_bwd, lru_wi_bwd, lru_bi_bwd, lru_lam_bwd, lru_norm_g, gla_wa_fwd, gla_ba_fwd, gla_wa_bwd, gla_ba_bwd, gla_norm_g, w_out, norm2_g, w_up, ffn_conv_w, ffn_conv_b, w_down, final_norm_g):
    nb, seq, d = x.shape
    ctx_len = ctx.shape[1]
    depth = w_ada.shape[0]
    assert ctx_len == TOKEN_BLOCK and seq % FFN_ROWS == 0
    rows_per_batch = ctx_len + seq
    nblk = rows_per_batch // TOKEN_BLOCK
    n_ctx_blk = ctx_len // TOKEN_BLOCK
    m = nb * rows_per_batch

    a_w = hgrn_norm_g.shape[1]
    b_w = lru_norm_g.shape[1]
    c_w = gla_norm_g.shape[1]
    c_k = gla_ba_fwd.shape[1]
    c_dk = c_k // C_HEADS
    c_kp = -(-c_dk // LANE) * LANE
    n_blocks = b_w // B_BW

    ta, tb_, tck, tcv = a_w // LANE, b_w // LANE, C_HEADS * c_kp // LANE, c_w // LANE
    off_bx = 5 * ta
    n_proj1 = (off_bx + tb_) * LANE
    off_bg = 0
    off_cq = -(-(off_bg + tb_) // tck) * tck
    off_ck = off_cq + tck
    off_cv = -(-(off_ck + tck) // tcv) * tcv
    off_cg = off_cv + tcv
    off_ab = off_cg + tcv
    tn_in = 512
    n_proj2 = -(-(off_ab + 1) * LANE // tn_in) * tn_in
    assert n_proj1 % tn_in == 0

    tm = _pick_tile(rows_per_batch, (768, 512, 256))
    tm_in = _pick_tile(rows_per_batch, (1408, 768, 512, 256))

    xs = jnp.concatenate([ctx, x], axis=1).reshape(m, d)

    c16 = jnp.zeros((16, d), F32).at[:nb].set(c).at[nb].set(c_ctx)
    mods = _ada(c16, w_ada, b_ada)
    lbf_all = _lower_bounds(hgrn_lb_fwd)
    lbb_all = _lower_bounds(hgrn_lb_bwd)

    def place(cols, parts):
        out, pos = [], 0
        for off, arr in sorted(parts, key=lambda p: p[0]):
            start = off * LANE
            if start > pos:
                out.append(jnp.zeros((d, start - pos), arr.dtype))
            out.append(arr)
            pos = start + arr.shape[1]
        if pos < cols:
            out.append(jnp.zeros((d, cols - pos), parts[0][1].dtype))
        return jnp.concatenate(out, axis=1)

    for l in range(depth):
        mod = mods[l, :nb + 1].reshape(nb + 1, 6, 1, d)
        sh1, sc1, g1, sh2, sc2, g2 = (mod[:, i] for i in range(6))

        wl = w_in[l]
        o = n_proj1
        sec = {}
        for name, size in (("bg", b_w), ("cq", c_k), ("ck", c_k), ("cv", c_w), ("cg", c_w), ("ab", 2 * C_RANK)):
            sec[name] = wl[:, o:o + size].astype(BF16)
            o += size
        w_proj1 = wl[:, :n_proj1].astype(BF16)
        w_proj2 = place(n_proj2, [
            (off_bg, sec["bg"]), (off_cq, _pad_heads(sec["cq"], C_HEADS, c_kp)),
            (off_ck, _pad_heads(sec["ck"], C_HEADS, c_kp)), (off_cv, sec["cv"]), (off_cg, sec["cg"]),
            (off_ab, sec["ab"]),
        ])

        h = _norm_mod(xs, norm1_g[l], sh1, sc1, nblk, n_ctx_blk)
        proj = _matmul(h, w_proj1, tm_in, tn_in, a_resident=True)
        proj2 = _matmul(h, w_proj2, tm_in, tn_in, a_resident=True)

        lbf = lbf_all[l].reshape(1, a_w)
        lbb = lbb_all[l].reshape(1, a_w)
        oa = _hgrn(proj, lbf, nb, nblk, reverse=False)
        ya = _hgrn(proj, lbb, nb, nblk, reverse=True, o_fwd=oa, gain=hgrn_norm_g[l].reshape(1, a_w))

        w_gates = jnp.concatenate([lru_wr_fwd[l], lru_wi_fwd[l], lru_wr_bwd[l], lru_wi_bwd[l]], axis=-1).astype(BF16)
        b_gates = jnp.concatenate([t[l].reshape(n_blocks, 1, B_BW) for t in
                                   (lru_br_fwd, lru_bi_fwd, lru_br_bwd, lru_bi_bwd)], axis=-1)
        lam = jnp.stack([lru_lam_fwd[l], lru_lam_bwd[l]], axis=0)
        hb = _lru(proj, off_bx, lru_conv_w[l], lru_conv_b[l].reshape(1, b_w), w_gates, b_gates, lam,
                  nb, nblk, n_ctx_blk)
        yb = _lru_fin(hb, proj2, off_bg, lru_norm_g[l].reshape(1, b_w))

        cols = (off_cq, off_ck, off_cv, off_cg, off_ab)
        kw = C_HEADS * c_kp

        def gate_params(wa, ba, row0):
            wp = jnp.zeros((LANE, kw), F32).at[row0:row0 + C_RANK].set(_pad_heads(wa, C_HEADS, c_kp))
            hi = wp.astype(BF16)
            lo = (wp - hi.astype(F32)).astype(BF16)
            w_bound = jnp.broadcast_to(jnp.max(jnp.abs(wp), axis=1, keepdims=True) * BOUND_MARGIN, (LANE, LANE))
            bias_bound = jnp.full((1, LANE), LOG2, F32) + jnp.max(jnp.abs(ba))
            return (jnp.concatenate([hi, hi, lo], axis=0), _pad_heads(ba, C_HEADS, c_kp).reshape(1, kw),
                    w_bound.astype(BF16), bias_bound)

        q_scale = c_dk ** -0.5
        gain_c = gla_norm_g[l].reshape(1, c_w)
        oc = _glac(proj2, cols, gate_params(gla_wa_fwd[l], gla_ba_fwd[l], 0), q_scale, c_w, nb, nblk, reverse=False)
        yc = _glac(proj2, cols, gate_params(gla_wa_bwd[l], gla_ba_bwd[l], C_RANK), q_scale, c_w, nb, nblk,
                   reverse=True, o_fwd=oc, gain=gain_c)

        wo = w_out[l]
        wo_parts = [wo[:a_w].astype(BF16), wo[a_w:a_w + b_w].astype(BF16), wo[a_w + b_w:].astype(BF16)]
        xs = _matmul_res([ya, yb, yc], wo_parts, xs, g1, tm, 1024, rows_per_batch, ctx_len)

        h2 = _norm_mod(xs, norm2_g[l], sh2, sc2, nblk, n_ctx_blk)
        up = _matmul(h2, w_up[l].astype(BF16), tm, 1024)
        act = _ffn_mid(up, ffn_conv_w[l].reshape(9, -1), ffn_conv_b[l].reshape(1, -1), nb, ctx_len)
        xs = _matmul_res([act], [w_down[l].astype(BF16)], xs, g2, tm, 1024, rows_per_batch, ctx_len)

    return _final_norm(xs, final_norm_g, nb, nblk, n_ctx_blk, seq)
```

```python
import functools

import jax
import jax.numpy as jnp
from jax import lax
from jax.experimental import pallas as pl
from jax.experimental.pallas import tpu as pltpu

F32 = jnp.float32
BF16 = jnp.bfloat16
HIGHEST = lax.Precision.HIGHEST

EPS = 1e-6
GRID_W = 64
LANE = 128
SUBLANE = 8
A_DK = 128
A_HEADS_PER_STEP = 12
B_BW = 128
B_CONV = 4
RG_C = 8.0
C_HEADS = 4
C_RANK = 16
GLA_NORMALIZER = 16.0
TOKEN_BLOCK = 256
GLA_CHUNK = 128
GLA_SUB = 32
LOG2 = 0.6931471805599453
TINY = 1e-30
BOUND_MARGIN = 1.02
SAFE_HALF_DECAY = 60.0
FFN_ROWS = 512
VMEM_LIMIT = 56 * 1024 * 1024


def _cp(*sem):
    return pltpu.CompilerParams(dimension_semantics=sem, vmem_limit_bytes=VMEM_LIMIT)


def _dot_nt(a, b):
    return lax.dot_general(a, b, (((1,), (1,)), ((), ())), preferred_element_type=F32)


def _dot_tn(a, b):
    return lax.dot_general(a, b, (((0,), (0,)), ((), ())), preferred_element_type=F32)


def _silu(v):
    return v * jax.nn.sigmoid(v)


def _ada_body(c_ref, w_ref, b_ref, o_ref):
    a = _silu(c_ref[...]).astype(BF16)
    o_ref[...] = jnp.dot(a, w_ref[...].astype(BF16), preferred_element_type=F32) + b_ref[...]


def _ada(c16, w_ada, b_ada, tn=512):
    depth, d, n = w_ada.shape
    rows = c16.shape[0]
    return pl.pallas_call(
        _ada_body,
        grid=(depth, n // tn),
        in_specs=[
            pl.BlockSpec((rows, d), lambda l, j: (0, 0)),
            pl.BlockSpec((None, d, tn), lambda l, j: (l, 0, j)),
            pl.BlockSpec((None, 1, tn), lambda l, j: (l, 0, j)),
        ],
        out_specs=pl.BlockSpec((None, rows, tn), lambda l, j: (l, 0, j)),
        out_shape=jax.ShapeDtypeStruct((depth, rows, n), F32),
        compiler_params=_cp("parallel", "parallel"),
        name="ada_mod",
    )(c16, w_ada, b_ada.reshape(depth, 1, n))


def _norm_mod_body(x_ref, g_ref, sh_ref, sc_ref, o_ref):
    x = x_ref[...]
    y = x * lax.rsqrt(jnp.mean(x * x, axis=-1, keepdims=True) + EPS) * g_ref[...]
    o_ref[...] = (y * (1.0 + sc_ref[...]) + sh_ref[...]).astype(o_ref.dtype)


def _norm_mod(x, g, shift, scale, nblk, n_ctx_blk):
    m, d = x.shape
    tr = TOKEN_BLOCK
    nb = shift.shape[0] - 1

    def mod_row(i):
        return jnp.where(i % nblk < n_ctx_blk, nb, i // nblk)

    return pl.pallas_call(
        _norm_mod_body,
        grid=(m // tr,),
        in_specs=[
            pl.BlockSpec((tr, d), lambda i: (i, 0)),
            pl.BlockSpec((1, d), lambda i: (0, 0)),
            pl.BlockSpec((None, 1, d), lambda i: (mod_row(i), 0, 0)),
            pl.BlockSpec((None, 1, d), lambda i: (mod_row(i), 0, 0)),
        ],
        out_specs=pl.BlockSpec((tr, d), lambda i: (i, 0)),
        out_shape=jax.ShapeDtypeStruct((m, d), BF16),
        compiler_params=_cp("parallel"),
        name="norm_mod",
    )(x, g.reshape(1, d), shift, scale)


def _cast_pad_body(w_ref, o_ref, *, n_valid):
    row = lax.broadcasted_iota(jnp.int32, w_ref.shape, 0) + pl.program_id(1) * w_ref.shape[0]
    o_ref[...] = jnp.where(row < n_valid, w_ref[...], 0.0).astype(o_ref.dtype)


def _cast_pad(wt, n_out, tn=512):
    depth, n, k = wt.shape
    return pl.pallas_call(
        functools.partial(_cast_pad_body, n_valid=n),
        grid=(depth, n_out // tn),
        in_specs=[pl.BlockSpec((None, tn, k), lambda l, j: (l, j, 0))],
        out_specs=pl.BlockSpec((None, tn, k), lambda l, j: (l, j, 0)),
        out_shape=jax.ShapeDtypeStruct((depth, n_out, k), BF16),
        compiler_params=_cp("parallel", "parallel"),
        name="cast_pad_weights",
    )(wt)


def _mm_body(a_ref, b_ref, o_ref, *, b_transposed):
    if b_transposed:
        acc = _dot_nt(a_ref[...], b_ref[...])
    else:
        acc = jnp.dot(a_ref[...], b_ref[...], preferred_element_type=F32)
    o_ref[...] = acc.astype(o_ref.dtype)


def _matmul(a, b, layer, tm, tn, out_dtype=F32, a_resident=False, b_transposed=False):
    m, k = a.shape
    n = b.shape[1] if b_transposed else b.shape[2]
    b_block = (None, tn, k) if b_transposed else (None, k, tn)
    if a_resident:
        grid = (m // tm, n // tn)
        a_map, o_map = (lambda i, j: (i, 0)), (lambda i, j: (i, j))
        b_map = (lambda i, j: (layer, j, 0)) if b_transposed else (lambda i, j: (layer, 0, j))
    else:
        grid = (n // tn, m // tm)
        a_map, o_map = (lambda j, i: (i, 0)), (lambda j, i: (i, j))
        b_map = (lambda j, i: (layer, j, 0)) if b_transposed else (lambda j, i: (layer, 0, j))
    return pl.pallas_call(
        functools.partial(_mm_body, b_transposed=b_transposed),
        grid=grid,
        in_specs=[pl.BlockSpec((tm, k), a_map), pl.BlockSpec(b_block, b_map)],
        out_specs=pl.BlockSpec((tm, tn), o_map),
        out_shape=jax.ShapeDtypeStruct((m, n), out_dtype),
        compiler_params=_cp("parallel", "parallel"),
        name="proj_matmul",
    )(a, b)


def _mm_res_body(*refs, n_ops, ctx_rows, tiles_per_batch):
    a_refs = refs[:n_ops]
    b_refs = refs[n_ops:2 * n_ops]
    r_ref, gb_ref, gc_ref, o_ref = refs[2 * n_ops:]
    acc = jnp.dot(a_refs[0][...], b_refs[0][...], preferred_element_type=F32)
    for a_ref, b_ref in zip(a_refs[1:], b_refs[1:]):
        acc += jnp.dot(a_ref[...], b_ref[...], preferred_element_type=F32)
    tm = acc.shape[0]
    first = pl.program_id(1) % tiles_per_batch == 0
    row = lax.broadcasted_iota(jnp.int32, (tm, 1), 0)
    is_ctx = jnp.logical_and(first, row < ctx_rows)
    gate = jnp.where(is_ctx, gc_ref[...], gb_ref[...])
    o_ref[...] = r_ref[...] + gate * acc


def _matmul_res(a_list, b_list, layer, resid, gate, tm, tn, rows_per_batch, ctx_rows):
    m, n = resid.shape
    nb = gate.shape[0] - 1
    tiles_per_batch = rows_per_batch // tm
    assert ctx_rows <= tm
    n_ops = len(a_list)
    in_specs = [pl.BlockSpec((tm, a.shape[1]), lambda j, i: (i, 0)) for a in a_list]
    in_specs += [pl.BlockSpec((None, b.shape[1], tn), lambda j, i: (layer, 0, j)) for b in b_list]
    in_specs += [
        pl.BlockSpec((tm, tn), lambda j, i: (i, j)),
        pl.BlockSpec((None, 1, tn), lambda j, i: (i // tiles_per_batch, 0, j)),
        pl.BlockSpec((None, 1, tn), lambda j, i: (nb, 0, j)),
    ]
    return pl.pallas_call(
        functools.partial(_mm_res_body, n_ops=n_ops, ctx_rows=ctx_rows, tiles_per_batch=tiles_per_batch),
        grid=(n // tn, m // tm),
        in_specs=in_specs,
        out_specs=pl.BlockSpec((tm, tn), lambda j, i: (i, j)),
        out_shape=jax.ShapeDtypeStruct((m, n), F32),
        compiler_params=_cp("parallel", "parallel"),
        name="res_matmul",
    )(*a_list, *b_list, resid, gate, gate)


def _split3(g):
    hi = g.astype(BF16)
    r = g - hi.astype(F32)
    mid = r.astype(BF16)
    lo = (r - mid.astype(F32)).astype(BF16)
    return jnp.concatenate([hi, mid, lo], axis=1)


def _blocks(rows_list):
    return jnp.concatenate([jnp.broadcast_to(r, (GLA_SUB, r.shape[1])) for r in rows_list], axis=0)


def _window_decay_may_be_large(lower_bound_rows, nrows):
    worst = None
    for r0 in range(0, nrows, GLA_SUB):
        part = None
        for r in range(r0, r0 + GLA_SUB, SUBLANE):
            rows = lower_bound_rows(r)
            part = rows if part is None else part + rows
        worst = part if worst is None else jnp.minimum(worst, part)
    return SUBLANE * jnp.min(worst) < -2.0 * SAFE_HALF_DECAY


def _diag_scores_safe(q, k, g, reverse):
    c = q.shape[0]
    kdim = q.shape[1]
    row = lax.broadcasted_iota(jnp.int32, (c, c), 0)
    col = lax.broadcasted_iota(jnp.int32, (c, c), 1)
    prow = lax.broadcasted_iota(jnp.int32, (c, kdim), 0)
    g3 = _split3(g)
    kb = k.astype(BF16)
    scores = jnp.where(row == col, _dot_nt(q.astype(BF16), kb), 0.0)
    w = 1
    while w < GLA_SUB:
        inblock = 2 * w - 1
        mid = (row & ~inblock) + w
        if reverse:
            take = ((col >= row) & (col < mid)) | ((row >= mid) & (col >= mid) & (col < row))
            is_query = (prow & inblock) < w
        else:
            take = ((col >= mid) & (col <= row)) | ((row < mid) & (col > row) & (col < mid))
            is_query = (prow & inblock) >= w
        ee = jnp.dot(jnp.where(take, 1.0, 0.0).astype(BF16), g3, preferred_element_type=F32)
        factor = jnp.exp(ee[:, :kdim] + ee[:, kdim:2 * kdim] + ee[:, 2 * kdim:])
        qm = jnp.where(is_query, q * factor, 0.0).astype(BF16)
        km = jnp.where(is_query, 0.0, k * factor).astype(BF16)
        same = (row & ~inblock) == (col & ~inblock)
        scores = scores + jnp.where(same, _dot_nt(qm, km), 0.0)
        w *= 2
    return scores


def _gla_chunks(qs, ks, vs, gs, sts, reverse, robust, safe_diags):
    c = qs[0].shape[0]
    nsub = c // GLA_SUB
    row = lax.broadcasted_iota(jnp.int32, (c, c), 0)
    col = lax.broadcasted_iota(jnp.int32, (c, c), 1)
    tri = jnp.where((col >= row) if reverse else (col <= row), 1.0, 0.0).astype(BF16)
    srow = lax.broadcasted_iota(jnp.int32, (GLA_SUB, c), 0)
    scol = lax.broadcasted_iota(jnp.int32, (GLA_SUB, c), 1)
    order = list(range(nsub - 1, -1, -1)) if reverse else list(range(nsub))

    bs = []
    for g in gs:
        kdim = g.shape[1]
        bb = jnp.dot(tri, _split3(g), preferred_element_type=F32)
        bs.append(bb[:, :kdim] + bb[:, kdim:2 * kdim] + bb[:, 2 * kdim:])

    work = []
    for q, k, b in zip(qs, ks, bs):
        zero = jnp.zeros((1, q.shape[1]), F32)
        entry, leave = [None] * nsub, [None] * nsub
        for i in range(nsub):
            r0, r1 = i * GLA_SUB, (i + 1) * GLA_SUB
            if reverse:
                entry[i] = b[r1:r1 + 1] if r1 < c else zero
                leave[i] = b[r0:r0 + 1]
            else:
                entry[i] = b[r0 - 1:r0] if r0 > 0 else zero
                leave[i] = b[r1 - 1:r1]
        edge = leave[order[-1]]
        qi = q * jnp.exp(b - _blocks(entry))
        kh = k * jnp.exp(_blocks(leave) - b)
        half = _blocks([jnp.exp(0.5 * (entry[i] - leave[i])) for i in range(nsub)])
        q_in = (qi * _blocks([jnp.exp(e) for e in entry])).astype(BF16)
        k_out = (kh * _blocks([jnp.exp(edge - x) for x in leave])).astype(BF16)
        qd = (qi * half).astype(BF16)
        kd = (kh * half).astype(BF16)
        pairs = []
        for pos, i in enumerate(order):
            for j in order[:pos]:
                r0 = i * GLA_SUB
                pairs.append((i, j, (qi[r0:r0 + GLA_SUB] * jnp.exp(entry[i] - leave[j])).astype(BF16)))
        work.append((q_in, k_out, qd, kd, kh.astype(BF16), pairs, edge))

    atts = []
    for (q_in, k_out, qd, kd, khb, pairs, edge), safe_diag in zip(work, safe_diags):
        diag = jnp.where(robust, safe_diag, _dot_nt(qd, kd))
        off = _dot_nt(jnp.concatenate([p[2] for p in pairs], axis=0), khb)
        blocks = []
        for i in range(nsub):
            r0 = i * GLA_SUB
            inside = (scol >= r0) & (scol < r0 + GLA_SUB)
            keep = inside & ((scol - r0 >= srow) if reverse else (scol - r0 <= srow))
            att = jnp.where(keep, diag[r0:r0 + GLA_SUB], 0.0)
            for n, (pi, pj, _) in enumerate(pairs):
                if pi == i:
                    c0 = pj * GLA_SUB
                    att = jnp.where((scol >= c0) & (scol < c0 + GLA_SUB), off[n * GLA_SUB:(n + 1) * GLA_SUB], att)
            blocks.append(att.astype(BF16))
        atts.append(jnp.concatenate(blocks, axis=0))

    outs, new_sts = [], []
    for (q_in, k_out, qd, kd, khb, pairs, edge), att, v, st in zip(work, atts, vs, sts):
        vb = v.astype(BF16)
        outs.append(jnp.dot(att, vb, preferred_element_type=F32) + _dot_nt(q_in, st.astype(BF16)))
        new_sts.append(st * jnp.exp(edge) + _dot_tn(vb, k_out))
    return outs, new_sts


def _seq_block(t, nblk, reverse):
    if not reverse:
        return t
    n_ctx = 1
    return jnp.where(t < n_ctx, n_ctx - 1 - t, nblk - 1 - (t - n_ctx))


def _hgrn_body(q_ref, z_ref, v_ref, lb_ref, *rest, reverse, finish):
    if finish:
        of_ref, g_ref, gain_ref, y_ref, st_ref, diag_ref = rest
    else:
        o_ref, st_ref, diag_ref = rest

    @pl.when(pl.program_id(2) == 0)
    def _():
        st_ref[...] = jnp.zeros_like(st_ref)
        diag_ref[...] = jnp.zeros_like(diag_ref)

    tb = q_ref.shape[0]
    heads = q_ref.shape[1] // A_DK
    nchunk = tb // GLA_CHUNK
    order = range(nchunk - 1, -1, -1) if reverse else range(nchunk)
    lb = lb_ref[...]
    cols = [slice(h * A_DK, (h + 1) * A_DK) for h in range(heads)]

    def prep(rs):
        qh = _silu(q_ref[rs, :]) * (A_DK ** -0.5)
        sig = jax.nn.sigmoid(z_ref[rs, :])
        key = (1.0 - lb) * (1.0 - sig)
        logf = jnp.log(lb + (1.0 - lb) * sig)
        return qh, key, logf

    robust = _window_decay_may_be_large(
        lambda r: jnp.minimum(z_ref[r:r + SUBLANE, :], 0.0) - LOG2, tb)

    @pl.when(robust)
    def _():
        for c in order:
            rs = slice(c * GLA_CHUNK, (c + 1) * GLA_CHUNK)
            qh, key, logf = prep(rs)
            for h, cs in enumerate(cols):
                diag_ref[c * heads + h] = _diag_scores_safe(qh[:, cs], key[:, cs], logf[:, cs], reverse)

    sts = [st_ref[h] for h in range(heads)]
    for c in order:
        rs = slice(c * GLA_CHUNK, (c + 1) * GLA_CHUNK)
        qh, key, logf = prep(rs)
        v = v_ref[rs, :]
        outs, sts = _gla_chunks([qh[:, cs] for cs in cols], [key[:, cs] for cs in cols], [v[:, cs] for cs in cols],
                                [logf[:, cs] for cs in cols], sts, reverse, robust,
                                [diag_ref[c * heads + h] for h in range(heads)])
        for cs, o in zip(cols, outs):
            if finish:
                o = o + of_ref[rs, cs]
                y = o * lax.rsqrt(jnp.mean(o * o, axis=-1, keepdims=True) + EPS)
                y_ref[rs, cs] = (y * gain_ref[:, cs] * _silu(g_ref[rs, cs])).astype(y_ref.dtype)
            else:
                o_ref[rs, cs] = o
    for h in range(heads):
        st_ref[h] = sts[h]


def _hgrn(proj, lb, nb, nblk, reverse, o_fwd=None, gain=None):
    m = proj.shape[0]
    width = lb.shape[1]
    hw = A_HEADS_PER_STEP * A_DK
    nhb = width // hw
    tb = TOKEN_BLOCK
    finish = o_fwd is not None

    def rows(b, hb, t):
        return b * nblk + _seq_block(t, nblk, reverse)

    def spec(section):
        return pl.BlockSpec((tb, hw), lambda b, hb, t: (rows(b, hb, t), section * nhb + hb))

    in_specs = [spec(0), spec(2 if reverse else 1), spec(3), pl.BlockSpec((1, hw), lambda b, hb, t: (0, hb))]
    args = [proj, proj, proj, lb]
    if finish:
        in_specs += [pl.BlockSpec((tb, hw), lambda b, hb, t: (rows(b, hb, t), hb)), spec(4),
                     pl.BlockSpec((1, hw), lambda b, hb, t: (0, hb))]
        args += [o_fwd, proj, gain]
        out_dtype = BF16
    else:
        out_dtype = F32
    return pl.pallas_call(
        functools.partial(_hgrn_body, reverse=reverse, finish=finish),
        grid=(nb, nhb, nblk),
        in_specs=in_specs,
        out_specs=pl.BlockSpec((tb, hw), lambda b, hb, t: (rows(b, hb, t), hb)),
        out_shape=jax.ShapeDtypeStruct((m, width), out_dtype),
        scratch_shapes=[pltpu.VMEM((A_HEADS_PER_STEP, A_DK, A_DK), F32),
                        pltpu.VMEM((A_HEADS_PER_STEP * (tb // GLA_CHUNK), GLA_CHUNK, GLA_CHUNK), F32)],
        compiler_params=_cp("parallel", "parallel", "arbitrary"),
        name="hgrn_bwd" if reverse else "hgrn_fwd",
    )(*args)


def _log_sigmoid(v):
    return jnp.minimum(v, 0.0) - jnp.log(1.0 + jnp.exp(-jnp.abs(v)))


def _col_specs(c0, width, rows_block, row_map):
    bw = next(w for w in (width, 1536, 1024, 768, 512, 256, LANE) if c0 % w == 0 and width % w == 0)
    return [pl.BlockSpec((rows_block, bw), functools.partial(lambda i, *g: (row_map(*g), c0 // bw + i), i))
            for i in range(width // bw)]


def _cat(refs, rs):
    parts = [r[rs, :] for r in refs]
    return parts[0] if len(parts) == 1 else jnp.concatenate(parts, axis=1)


def _glac_body(*refs, counts, dk, reverse, finish, q_scale):
    nq, nk, nv, ng = counts
    q_refs, refs = refs[:nq], refs[nq:]
    k_refs, refs = refs[:nk], refs[nk:]
    v_refs, refs = refs[:nv], refs[nv:]
    a_ref, wa_ref, ba_ref, wb_ref, fb_ref = refs[:5]
    refs = refs[5:]
    if finish:
        of_ref, refs = refs[0], refs[1:]
        g_refs, refs = refs[:ng], refs[ng:]
        gain_ref, y_ref, st_ref, diag_ref = refs
    else:
        o_ref, st_ref, diag_ref = refs

    @pl.when(pl.program_id(1) == 0)
    def _():
        st_ref[...] = jnp.zeros_like(st_ref)
        diag_ref[...] = jnp.zeros_like(diag_ref)

    tb = a_ref.shape[0]
    kp = st_ref.shape[2]
    vw = st_ref.shape[1]
    nchunk = tb // GLA_CHUNK
    order = range(nchunk - 1, -1, -1) if reverse else range(nchunk)
    a = a_ref[...]
    reach = jnp.dot(jnp.abs(a).astype(BF16), wb_ref[...], preferred_element_type=F32) + fb_ref[...]
    floor = reach * (-1.0 / GLA_NORMALIZER)
    robust = _window_decay_may_be_large(lambda r: floor[r:r + SUBLANE], tb)
    a_hi = a.astype(BF16)
    a_lo = (a - a_hi.astype(F32)).astype(BF16)
    logits = jnp.dot(jnp.concatenate([a_hi, a_lo, a_hi], axis=1), wa_ref[...],
                     preferred_element_type=F32) + ba_ref[...]
    starts = [(h * dk // LANE) * LANE for h in range(C_HEADS)]
    wins = [slice(s, s + kp) for s in starts]
    lane = lax.broadcasted_iota(jnp.int32, (1, kp), 1)
    valid = [(lane >= h * dk - s) & (lane < (h + 1) * dk - s) for h, s in enumerate(starts)]
    vcols = [slice(h * vw, (h + 1) * vw) for h in range(C_HEADS)]

    def operands(rs):
        q = _cat(q_refs, rs) * q_scale
        k = _cat(k_refs, rs)
        gate = _log_sigmoid(logits[rs]) * (1.0 / GLA_NORMALIZER)
        qs = [jnp.where(ok, q[:, w], 0.0) for ok, w in zip(valid, wins)]
        ks = [jnp.where(ok, k[:, w], 0.0) for ok, w in zip(valid, wins)]
        return qs, ks, [gate[:, w] for w in wins]

    @pl.when(robust)
    def _():
        for c in order:
            qs, ks, gs = operands(slice(c * GLA_CHUNK, (c + 1) * GLA_CHUNK))
            for h in range(C_HEADS):
                diag_ref[c * C_HEADS + h] = _diag_scores_safe(qs[h], ks[h], gs[h], reverse)

    states = [st_ref[h] for h in range(C_HEADS)]
    for c in order:
        rs = slice(c * GLA_CHUNK, (c + 1) * GLA_CHUNK)
        qs, ks, gs = operands(rs)
        v = _cat(v_refs, rs)
        outs, states = _gla_chunks(qs, ks, [v[:, vs] for vs in vcols], gs, states, reverse,
                                   robust, [diag_ref[c * C_HEADS + h] for h in range(C_HEADS)])
        if finish:
            g = _cat(g_refs, rs)
        for vs, o in zip(vcols, outs):
            if finish:
                o = o + of_ref[rs, vs]
                y = o * lax.rsqrt(jnp.mean(o * o, axis=-1, keepdims=True) + EPS)
                y_ref[rs, vs] = (y * gain_ref[:, vs] * _silu(g[:, vs])).astype(y_ref.dtype)
            else:
                o_ref[rs, vs] = o
    for h in range(C_HEADS):
        st_ref[h] = states[h]


def _glac(proj, cols, gate_params, dk, vw, nb, nblk, reverse, o_fwd=None, gain=None):
    wa_st, ba, w_bound, bias_bound = gate_params
    m = proj.shape[0]
    kw = ba.shape[1]
    kp = max(-(-((h * dk) % LANE + dk) // LANE) * LANE for h in range(C_HEADS))
    tb = TOKEN_BLOCK
    finish = o_fwd is not None
    q_c, k_c, v_c, g_c, a_c = cols

    def rows(b, t):
        return b * nblk + _seq_block(t, nblk, reverse)

    def const(shape):
        return pl.BlockSpec(shape, lambda b, t: (0, 0))

    q_specs, k_specs, v_specs = (_col_specs(c0, w, tb, rows) for c0, w in ((q_c, kw), (k_c, kw), (v_c, vw)))
    g_specs = _col_specs(g_c, vw, tb, rows) if finish else []
    in_specs = q_specs + k_specs + v_specs + _col_specs(a_c, LANE, tb, rows)
    in_specs += [const((3 * LANE, kw)), const((1, kw)), const((LANE, LANE)), const((1, LANE))]
    args = [proj] * (len(q_specs) + len(k_specs) + len(v_specs) + 1) + [wa_st, ba, w_bound, bias_bound]
    if finish:
        in_specs += [pl.BlockSpec((tb, vw), lambda b, t: (rows(b, t), 0))] + g_specs + [const((1, vw))]
        args += [o_fwd] + [proj] * len(g_specs) + [gain]
    return pl.pallas_call(
        functools.partial(_glac_body, counts=(len(q_specs), len(k_specs), len(v_specs), len(g_specs)), dk=dk,
                          reverse=reverse, finish=finish, q_scale=dk ** -0.5),
        grid=(nb, nblk),
        in_specs=in_specs,
        out_specs=pl.BlockSpec((tb, vw), lambda b, t: (rows(b, t), 0)),
        out_shape=jax.ShapeDtypeStruct((m, vw), BF16 if finish else F32),
        scratch_shapes=[pltpu.VMEM((C_HEADS, vw // C_HEADS, kp), F32),
                        pltpu.VMEM((C_HEADS * (tb // GLA_CHUNK), GLA_CHUNK, GLA_CHUNK), F32)],
        compiler_params=_cp("parallel", "arbitrary"),
        name="gla_bwd" if reverse else "gla_fwd",
    )(*args)


def _lru_body(x_ref, cw_ref, cb_ref, w_ref, bias_ref, lam_ref, o_ref, xc_ref, *, nblk, n_ctx_blk):
    rb = TOKEN_BLOCK
    total = x_ref.shape[0]
    gw = x_ref.shape[1]
    rid = lax.broadcasted_iota(jnp.int32, (SUBLANE, gw), 0)
    cw = cw_ref[...]
    cb = cb_ref[...]

    def conv(blk, start):
        cur = x_ref[pl.ds(start, rb), :]
        seg_start = jnp.logical_or(blk == 0, blk == n_ctx_blk)
        seg_end = jnp.logical_or(blk == n_ctx_blk - 1, blk == nblk - 1)
        pstart = pl.multiple_of(jnp.maximum(start - SUBLANE, 0), SUBLANE)
        nstart = pl.multiple_of(jnp.minimum(start + rb, total - SUBLANE), SUBLANE)
        prev8 = jnp.where(seg_start, 0.0, x_ref[pl.ds(pstart, SUBLANE), :])
        next8 = jnp.where(seg_end, 0.0, x_ref[pl.ds(nstart, SUBLANE), :])
        ext = jnp.concatenate([prev8, cur, next8], axis=0)
        n_ext = rb + 2 * SUBLANE

        def tap(shift):
            return pltpu.roll(ext, shift % n_ext, axis=0)[SUBLANE:SUBLANE + rb]

        return cb + cw[0:1] * tap(2) + cw[1:2] * tap(1) + cw[2:3] * cur + cw[3:4] * tap(-1)

    def run(reverse):
        wsl = slice(2 * gw, 4 * gw) if reverse else slice(0, 2 * gw)
        lam = lam_ref[1:2] if reverse else lam_ref[0:1]
        neg_c_softplus = -RG_C * (jnp.maximum(-lam, 0.0) + jnp.log(1.0 + jnp.exp(-jnp.abs(lam))))
        w = w_ref[:, wsl]
        bias = bias_ref[:, wsl]

        def stage(j):
            blk = _seq_block(j, nblk, reverse)
            start = pl.multiple_of(blk * rb, rb)
            if reverse:
                xc = xc_ref[pl.ds(start, rb), :]
            else:
                xc = conv(blk, start)
                xc_ref[pl.ds(start, rb), :] = xc
            return xc, jnp.dot(xc.astype(BF16), w, preferred_element_type=F32) + bias

        def body(j, carry):
            if reverse:
                h, xc, gates = carry
                xc_next, gates_next = stage(jnp.minimum(j + 1, nblk - 1))
            else:
                h = carry
                xc, gates = stage(j)
            blk = _seq_block(j, nblk, reverse)
            start = pl.multiple_of(blk * rb, rb)
            r = jax.nn.sigmoid(gates[:, :gw])
            i = jax.nn.sigmoid(gates[:, gw:])
            a = jnp.exp(neg_c_softplus * r)
            y = 1.0 - a * a
            u = (y * lax.rsqrt(jnp.maximum(y, TINY))) * (i * xc)
            ngroups = rb // SUBLANE
            order = range(ngroups - 1, -1, -1) if reverse else range(ngroups)
            last = slice(0, 1) if reverse else slice(SUBLANE - 1, SUBLANE)
            scanned = [None] * ngroups
            for gi in order:
                gs = slice(gi * SUBLANE, (gi + 1) * SUBLANE)
                ag, ug = a[gs], u[gs]
                for d in (1, 2, 4):
                    keep = (rid < SUBLANE - d) if reverse else (rid >= d)
                    shift = SUBLANE - d if reverse else d
                    a_s = jnp.where(keep, pltpu.roll(ag, shift, axis=0), 1.0)
                    u_s = jnp.where(keep, pltpu.roll(ug, shift, axis=0), 0.0)
                    ug = ag * u_s + ug
                    ag = ag * a_s
                scanned[gi] = (ag, ug)
            carry = [None] * ngroups
            for gi in order:
                ag, ug = scanned[gi]
                carry[gi] = h
                h = ag[last] * h + ug[last]
            hblk = jnp.concatenate([scanned[gi][1] + scanned[gi][0] * carry[gi] for gi in range(ngroups)], axis=0)
            if reverse:
                o_ref[pl.ds(start, rb), :] = o_ref[pl.ds(start, rb), :] + hblk
            else:
                o_ref[pl.ds(start, rb), :] = hblk
            return (h, xc_next, gates_next) if reverse else h

        h0 = jnp.zeros((1, gw), F32)
        lax.fori_loop(0, nblk, body, (h0,) + stage(0) if reverse else h0)

    run(False)
    run(True)


def _lru(proj, x_off, conv_w, conv_b, w_gates, b_gates, lam, nb, nblk, n_ctx_blk):
    m = proj.shape[0]
    nblocks = w_gates.shape[0]
    total = m // nb
    gw = B_BW
    return pl.pallas_call(
        functools.partial(_lru_body, nblk=nblk, n_ctx_blk=n_ctx_blk),
        grid=(nb, nblocks),
        in_specs=[
            pl.BlockSpec((total, gw), lambda b, n: (b, x_off + n)),
            pl.BlockSpec((B_CONV, gw), lambda b, n: (0, n)),
            pl.BlockSpec((1, gw), lambda b, n: (0, n)),
            pl.BlockSpec((None, gw, 4 * gw), lambda b, n: (n, 0, 0)),
            pl.BlockSpec((None, 1, 4 * gw), lambda b, n: (n, 0, 0)),
            pl.BlockSpec((2, gw), lambda b, n: (0, n)),
        ],
        out_specs=pl.BlockSpec((total, gw), lambda b, n: (b, n)),
        out_shape=jax.ShapeDtypeStruct((m, nblocks * gw), F32),
        scratch_shapes=[pltpu.VMEM((total, gw), F32)],
        compiler_params=_cp("parallel", "parallel"),
        name="rglru_scan",
    )(proj, conv_w, conv_b, w_gates, b_gates, lam)


def _gelu_tanh(v):
    return 0.5 * v * (1.0 + jnp.tanh(0.7978845608028654 * (v + 0.044715 * (v * v * v))))


def _lru_fin_body(h_ref, *refs):
    gate_refs, (gain_ref, y_ref) = refs[:-2], refs[-2:]
    h = h_ref[...]
    y = h * lax.rsqrt(jnp.mean(h * h, axis=-1, keepdims=True) + EPS) * gain_ref[...]
    y_ref[...] = (y * _gelu_tanh(_cat(gate_refs, slice(None)))).astype(y_ref.dtype)


def _lru_fin(h, proj, gate_col, gain):
    m, w = h.shape
    tr = TOKEN_BLOCK
    gate_specs = _col_specs(gate_col, w, tr, lambda i: i)
    return pl.pallas_call(
        _lru_fin_body,
        grid=(m // tr,),
        in_specs=[pl.BlockSpec((tr, w), lambda i: (i, 0))] + gate_specs + [pl.BlockSpec((1, w), lambda i: (0, 0))],
        out_specs=pl.BlockSpec((tr, w), lambda i: (i, 0)),
        out_shape=jax.ShapeDtypeStruct((m, w), BF16),
        compiler_params=_cp("parallel"),
        name="rglru_finish",
    )(h, *([proj] * len(gate_specs)), gain)


def _conv_rows(up, mid, down, w, bias, u, width):
    n, tc = mid.shape
    left = w[3:4] * mid
    cent = w[4:5] * mid
    right = w[5:6] * mid
    if up is not None:
        left += w[0:1] * up
        cent += w[1:2] * up
        right += w[2:3] * up
    if down is not None:
        left += w[6:7] * down
        cent += w[7:8] * down
        right += w[8:9] * down
    cidx = lax.broadcasted_iota(jnp.int32, (n, tc), 0) & (width - 1)
    acc = cent + bias
    acc += jnp.where(cidx != 0, pltpu.roll(left, 1, axis=0), 0.0)
    acc += jnp.where(cidx != width - 1, pltpu.roll(right, n - 1, axis=0), 0.0)
    return (_silu(acc) * u).astype(BF16)


def _ffn_mid_body(u_ref, g_ref, w_ref, b_ref, o_ref, *, ctx_rows):
    total = u_ref.shape[0]
    w = w_ref[...]
    bias = b_ref[...]
    o_ref[0:ctx_rows, :] = _conv_rows(None, g_ref[0:ctx_rows, :], None, w, bias, u_ref[0:ctx_rows, :], ctx_rows)
    step = FFN_ROWS
    nsteps = (total - ctx_rows) // step
    zeros = jnp.zeros((GRID_W, u_ref.shape[1]), F32)

    def load(start, n):
        return g_ref[pl.ds(start, n), :]

    def emit(start, up, mid, down):
        o_ref[pl.ds(start, step), :] = _conv_rows(up, mid, down, w, bias, u_ref[pl.ds(start, step), :], GRID_W)

    first = ctx_rows
    last = ctx_rows + (nsteps - 1) * step
    if nsteps == 1:
        mid = load(first, step)
        emit(first, jnp.concatenate([zeros, mid[:step - GRID_W]], axis=0), mid,
             jnp.concatenate([mid[GRID_W:], zeros], axis=0))
        return
    mid = load(first, step)
    emit(first, jnp.concatenate([zeros, mid[:step - GRID_W]], axis=0), mid, load(first + GRID_W, step))

    def body(s, carry):
        start = pl.multiple_of(ctx_rows + s * step, GRID_W)
        emit(start, load(start - GRID_W, step), load(start, step), load(start + GRID_W, step))
        return carry

    lax.fori_loop(1, nsteps - 1, body, 0)
    mid = load(last, step)
    emit(last, load(last - GRID_W, step), mid, jnp.concatenate([mid[GRID_W:], zeros], axis=0))


def _ffn_mid(up, conv_w9, conv_b, nb, ctx_rows, tc=LANE):
    m, two_f = up.shape
    f = two_f // 2
    total = m // nb
    nct = f // tc
    return pl.pallas_call(
        functools.partial(_ffn_mid_body, ctx_rows=ctx_rows),
        grid=(nb, nct),
        in_specs=[
            pl.BlockSpec((total, tc), lambda b, j: (b, j)),
            pl.BlockSpec((total, tc), lambda b, j: (b, nct + j)),
            pl.BlockSpec((9, tc), lambda b, j: (0, j)),
            pl.BlockSpec((1, tc), lambda b, j: (0, j)),
        ],
        out_specs=pl.BlockSpec((total, tc), lambda b, j: (b, j)),
        out_shape=jax.ShapeDtypeStruct((m, f), BF16),
        compiler_params=_cp("parallel", "parallel"),
        name="ffn_conv_gate",
    )(up, up, conv_w9, conv_b)


def _final_norm_body(x_ref, g_ref, o_ref):
    x = x_ref[...]
    o_ref[...] = x * lax.rsqrt(jnp.mean(x * x, axis=-1, keepdims=True) + EPS) * g_ref[...]


def _final_norm(x, g, nb, nblk, n_ctx_blk, seq):
    d = x.shape[1]
    tr = TOKEN_BLOCK
    return pl.pallas_call(
        _final_norm_body,
        grid=(nb, seq // tr),
        in_specs=[
            pl.BlockSpec((tr, d), lambda b, i: (b * nblk + n_ctx_blk + i, 0)),
            pl.BlockSpec((1, d), lambda b, i: (0, 0)),
        ],
        out_specs=pl.BlockSpec((None, tr, d), lambda b, i: (b, i, 0)),
        out_shape=jax.ShapeDtypeStruct((nb, seq, d), F32),
        compiler_params=_cp("parallel", "parallel"),
        name="final_norm",
    )(x, g.reshape(1, d))


def _lower_bounds(logits):
    p = jax.nn.softmax(logits.astype(F32), axis=0)
    return jnp.clip(jnp.cumsum(p, axis=0) - p[0], 0.0, 1.0)


def _pad_heads(w, heads, width):
    lead = w.shape[:-1]
    wh = w.shape[-1] // heads
    w = w.reshape(lead + (heads, wh))
    w = jnp.pad(w, [(0, 0)] * len(lead) + [(0, 0), (0, width - wh)])
    return w.reshape(lead + (heads * width,))


def _pick_tile(total, candidates):
    for t in candidates:
        if total % t == 0:
            return t
    raise ValueError(f"no tile for {total}")


def kernel(x, c, ctx, c_ctx, w_ada, b_ada, norm1_g, w_in, hgrn_lb_fwd, hgrn_lb_bwd, hgrn_norm_g, lru_conv_w, lru_conv_b, lru_wr_fwd, lru_br_fwd, lru_wi_fwd, lru_bi_fwd, lru_lam_fwd, lru_wr_bwd, lru_br_bwd, lru_wi_bwd, lru_bi_bwd, lru_lam_bwd, lru_norm_g, gla_wa_fwd, gla_ba_fwd, gla_wa_bwd, gla_ba_bwd, gla_norm_g, w_out, norm2_g, w_up, ffn_conv_w, ffn_conv_b, w_down, final_norm_g):
    nb, seq, d = x.shape
    ctx_len = ctx.shape[1]
    depth = w_ada.shape[0]
    assert ctx_len == TOKEN_BLOCK and seq % FFN_ROWS == 0
    rows_per_batch = ctx_len + seq
    nblk = rows_per_batch // TOKEN_BLOCK
    n_ctx_blk = ctx_len // TOKEN_BLOCK
    m = nb * rows_per_batch

    a_w = hgrn_norm_g.shape[1]
    b_w = lru_norm_g.shape[1]
    c_w = gla_norm_g.shape[1]
    c_k = gla_ba_fwd.shape[1]
    c_dk = c_k // C_HEADS
    n_blocks = b_w // B_BW

    col_bx = 5 * a_w
    col_bg = col_bx + b_w
    col_cq = col_bg + b_w
    col_ck = col_cq + c_k
    col_cv = col_ck + c_k
    col_cg = col_cv + c_w
    col_ab = col_cg + c_w
    assert col_ab % LANE == 0 and col_ab + 2 * C_RANK == w_in.shape[2]
    tn_in = 512
    n_proj = -(-w_in.shape[2] // tn_in) * tn_in

    tm = _pick_tile(rows_per_batch, (768, 512, 256))
    tm_in = _pick_tile(rows_per_batch, (1408, 768, 512, 256))

    xs = jnp.concatenate([ctx, x], axis=1).reshape(m, d)

    c16 = jnp.zeros((16, d), F32).at[:nb].set(c).at[nb].set(c_ctx)
    mods = _ada(c16, w_ada, b_ada)
    lbf_all = _lower_bounds(hgrn_lb_fwd)
    lbb_all = _lower_bounds(hgrn_lb_bwd)

    w_in_b = _cast_pad(jnp.swapaxes(w_in, 1, 2), n_proj)
    w_up_b = w_up.astype(BF16)
    w_down_b = w_down.astype(BF16)
    wo_parts = [w_out[:, :a_w].astype(BF16), w_out[:, a_w:a_w + b_w].astype(BF16), w_out[:, a_w + b_w:].astype(BF16)]

    def gate_params(wa, ba, row0):
        wp = jnp.zeros((LANE, c_k), F32).at[row0:row0 + C_RANK].set(wa)
        hi = wp.astype(BF16)
        lo = (wp - hi.astype(F32)).astype(BF16)
        w_bound = jnp.broadcast_to(jnp.max(jnp.abs(wp), axis=1, keepdims=True) * BOUND_MARGIN, (LANE, LANE))
        bias_bound = jnp.full((1, LANE), LOG2, F32) + jnp.max(jnp.abs(ba))
        return jnp.concatenate([hi, hi, lo], axis=0), ba.reshape(1, c_k), w_bound.astype(BF16), bias_bound

    for l in range(depth):
        mod = mods[l, :nb + 1].reshape(nb + 1, 6, 1, d)
        sh1, sc1, g1, sh2, sc2, g2 = (mod[:, i] for i in range(6))

        h = _norm_mod(xs, norm1_g[l], sh1, sc1, nblk, n_ctx_blk)
        proj = _matmul(h, w_in_b, l, tm_in, tn_in, a_resident=True, b_transposed=True)

        lbf = lbf_all[l].reshape(1, a_w)
        lbb = lbb_all[l].reshape(1, a_w)
        oa = _hgrn(proj, lbf, nb, nblk, reverse=False)
        ya = _hgrn(proj, lbb, nb, nblk, reverse=True, o_fwd=oa, gain=hgrn_norm_g[l].reshape(1, a_w))

        w_gates = jnp.concatenate([lru_wr_fwd[l], lru_wi_fwd[l], lru_wr_bwd[l], lru_wi_bwd[l]], axis=-1).astype(BF16)
        b_gates = jnp.concatenate([t[l].reshape(n_blocks, 1, B_BW) for t in
                                   (lru_br_fwd, lru_bi_fwd, lru_br_bwd, lru_bi_bwd)], axis=-1)
        lam = jnp.stack([lru_lam_fwd[l], lru_lam_bwd[l]], axis=0)
        hb = _lru(proj, col_bx // LANE, lru_conv_w[l], lru_conv_b[l].reshape(1, b_w), w_gates, b_gates, lam,
                  nb, nblk, n_ctx_blk)
        yb = _lru_fin(hb, proj, col_bg, lru_norm_g[l].reshape(1, b_w))

        cols = (col_cq, col_ck, col_cv, col_cg, col_ab)
        gain_c = gla_norm_g[l].reshape(1, c_w)
        oc = _glac(proj, cols, gate_params(gla_wa_fwd[l], gla_ba_fwd[l], 0), c_dk, c_w, nb, nblk, reverse=False)
        yc = _glac(proj, cols, gate_params(gla_wa_bwd[l], gla_ba_bwd[l], C_RANK), c_dk, c_w, nb, nblk,
                   reverse=True, o_fwd=oc, gain=gain_c)

        xs = _matmul_res([ya, yb, yc], wo_parts, l, xs, g1, tm, 1024, rows_per_batch, ctx_len)

        h2 = _norm_mod(xs, norm2_g[l], sh2, sc2, nblk, n_ctx_blk)
        up = _matmul(h2, w_up_b, l, tm, 1024)
        act = _ffn_mid(up, ffn_conv_w[l].reshape(9, -1), ffn_conv_b[l].reshape(1, -1), nb, ctx_len)
        xs = _matmul_res([act], [w_down_b], l, xs, g2, tm, 1024, rows_per_batch, ctx_len)

    return _final_norm(xs, final_norm_g, nb, nblk, n_ctx_blk, seq)
```

```python
import functools

import jax
import jax.numpy as jnp
from jax import lax
from jax.experimental import pallas as pl
from jax.experimental.pallas import tpu as pltpu

F32 = jnp.float32
BF16 = jnp.bfloat16

EPS = 1e-6
GRID_W = 64
LANE = 128
SUBLANE = 8
A_DK = 128
A_HEADS_PER_STEP = 12
B_BW = 128
B_CONV = 4
RG_C = 8.0
C_HEADS = 4
C_RANK = 16
GLA_NORMALIZER = 16.0
TOKEN_BLOCK = 256
GLA_CHUNK = 128
GLA_SUB = 32
LOG2 = 0.6931471805599453
TINY = 1e-30
BOUND_MARGIN = 1.02
SAFE_HALF_DECAY = 60.0
FFN_ROWS = 512
V7X_VMEM_BYTES = 64 * 1024 * 1024
VMEM_LIMIT = V7X_VMEM_BYTES * 7 // 8


def _cp(*sem):
    return pltpu.CompilerParams(dimension_semantics=sem, vmem_limit_bytes=VMEM_LIMIT)


def _dot_nt(a, b):
    return lax.dot_general(a, b, (((1,), (1,)), ((), ())), preferred_element_type=F32)


def _dot_tn(a, b):
    return lax.dot_general(a, b, (((0,), (0,)), ((), ())), preferred_element_type=F32)


def _silu(v):
    return v * jax.nn.sigmoid(v)


def _ada_body(c_ref, w_ref, b_ref, o_ref):
    a = _silu(c_ref[...]).astype(BF16)
    o_ref[...] = jnp.dot(a, w_ref[...].astype(BF16), preferred_element_type=F32) + b_ref[...]


def _ada(c16, w_ada, b_ada, tn=512):
    depth, d, n = w_ada.shape
    rows = c16.shape[0]
    return pl.pallas_call(
        _ada_body,
        grid=(depth, n // tn),
        in_specs=[
            pl.BlockSpec((rows, d), lambda l, j: (0, 0)),
            pl.BlockSpec((None, d, tn), lambda l, j: (l, 0, j)),
            pl.BlockSpec((None, 1, tn), lambda l, j: (l, 0, j)),
        ],
        out_specs=pl.BlockSpec((None, rows, tn), lambda l, j: (l, 0, j)),
        out_shape=jax.ShapeDtypeStruct((depth, rows, n), F32),
        compiler_params=_cp("parallel", "parallel"),
        name="ada_mod",
    )(c16, w_ada, b_ada.reshape(depth, 1, n))


def _norm_mod_body(x_ref, g_ref, sh_ref, sc_ref, o_ref):
    x = x_ref[...]
    y = x * lax.rsqrt(jnp.mean(x * x, axis=-1, keepdims=True) + EPS) * g_ref[...]
    o_ref[...] = (y * (1.0 + sc_ref[...]) + sh_ref[...]).astype(o_ref.dtype)


def _norm_mod(x, g, shift, scale, nblk, n_ctx_blk):
    m, d = x.shape
    tr = TOKEN_BLOCK
    nb = shift.shape[0] - 1

    def mod_row(i):
        return jnp.where(i % nblk < n_ctx_blk, nb, i // nblk)

    return pl.pallas_call(
        _norm_mod_body,
        grid=(m // tr,),
        in_specs=[
            pl.BlockSpec((tr, d), lambda i: (i, 0)),
            pl.BlockSpec((1, d), lambda i: (0, 0)),
            pl.BlockSpec((None, 1, d), lambda i: (mod_row(i), 0, 0)),
            pl.BlockSpec((None, 1, d), lambda i: (mod_row(i), 0, 0)),
        ],
        out_specs=pl.BlockSpec((tr, d), lambda i: (i, 0)),
        out_shape=jax.ShapeDtypeStruct((m, d), BF16),
        compiler_params=_cp("parallel"),
        name="norm_mod",
    )(x, g.reshape(1, d), shift, scale)


def _cast_pad_body(w_ref, o_ref, *, n_valid):
    row = lax.broadcasted_iota(jnp.int32, w_ref.shape, 0) + pl.program_id(1) * w_ref.shape[0]
    o_ref[...] = jnp.where(row < n_valid, w_ref[...], 0.0).astype(o_ref.dtype)


def _cast_pad(wt, n_out, tn=512):
    depth, n, k = wt.shape
    return pl.pallas_call(
        functools.partial(_cast_pad_body, n_valid=n),
        grid=(depth, n_out // tn),
        in_specs=[pl.BlockSpec((None, tn, k), lambda l, j: (l, j, 0))],
        out_specs=pl.BlockSpec((None, tn, k), lambda l, j: (l, j, 0)),
        out_shape=jax.ShapeDtypeStruct((depth, n_out, k), BF16),
        compiler_params=_cp("parallel", "parallel"),
        name="cast_pad_weights",
    )(wt)


def _mm_body(a_ref, b_ref, o_ref, *, b_transposed):
    if b_transposed:
        acc = _dot_nt(a_ref[...], b_ref[...])
    else:
        acc = jnp.dot(a_ref[...], b_ref[...], preferred_element_type=F32)
    o_ref[...] = acc.astype(o_ref.dtype)


def _matmul(a, b, layer, tm, tn, out_dtype=F32, a_resident=False, b_transposed=False):
    m, k = a.shape
    n = b.shape[1] if b_transposed else b.shape[2]
    b_block = (None, tn, k) if b_transposed else (None, k, tn)
    if a_resident:
        grid = (m // tm, n // tn)
        a_map, o_map = (lambda i, j: (i, 0)), (lambda i, j: (i, j))
        b_map = (lambda i, j: (layer, j, 0)) if b_transposed else (lambda i, j: (layer, 0, j))
    else:
        grid = (n // tn, m // tm)
        a_map, o_map = (lambda j, i: (i, 0)), (lambda j, i: (i, j))
        b_map = (lambda j, i: (layer, j, 0)) if b_transposed else (lambda j, i: (layer, 0, j))
    return pl.pallas_call(
        functools.partial(_mm_body, b_transposed=b_transposed),
        grid=grid,
        in_specs=[pl.BlockSpec((tm, k), a_map), pl.BlockSpec(b_block, b_map)],
        out_specs=pl.BlockSpec((tm, tn), o_map),
        out_shape=jax.ShapeDtypeStruct((m, n), out_dtype),
        compiler_params=_cp("parallel", "parallel"),
        name="proj_matmul",
    )(a, b)


def _mm_res_body(*refs, n_ops, ctx_rows, tiles_per_batch):
    a_refs = refs[:n_ops]
    b_refs = refs[n_ops:2 * n_ops]
    r_ref, gb_ref, gc_ref, o_ref = refs[2 * n_ops:]
    acc = jnp.dot(a_refs[0][...], b_refs[0][...], preferred_element_type=F32)
    for a_ref, b_ref in zip(a_refs[1:], b_refs[1:]):
        acc += jnp.dot(a_ref[...], b_ref[...], preferred_element_type=F32)
    tm = acc.shape[0]
    first = pl.program_id(1) % tiles_per_batch == 0
    row = lax.broadcasted_iota(jnp.int32, (tm, 1), 0)
    is_ctx = jnp.logical_and(first, row < ctx_rows)
    gate = jnp.where(is_ctx, gc_ref[...], gb_ref[...])
    o_ref[...] = r_ref[...] + gate * acc


def _matmul_res(a_list, b_list, layer, resid, gate, tm, tn, rows_per_batch, ctx_rows):
    m, n = resid.shape
    nb = gate.shape[0] - 1
    tiles_per_batch = rows_per_batch // tm
    assert ctx_rows <= tm
    n_ops = len(a_list)
    in_specs = [pl.BlockSpec((tm, a.shape[1]), lambda j, i: (i, 0)) for a in a_list]
    in_specs += [pl.BlockSpec((None, b.shape[1], tn), lambda j, i: (layer, 0, j)) for b in b_list]
    in_specs += [
        pl.BlockSpec((tm, tn), lambda j, i: (i, j)),
        pl.BlockSpec((None, 1, tn), lambda j, i: (i // tiles_per_batch, 0, j)),
        pl.BlockSpec((None, 1, tn), lambda j, i: (nb, 0, j)),
    ]
    return pl.pallas_call(
        functools.partial(_mm_res_body, n_ops=n_ops, ctx_rows=ctx_rows, tiles_per_batch=tiles_per_batch),
        grid=(n // tn, m // tm),
        in_specs=in_specs,
        out_specs=pl.BlockSpec((tm, tn), lambda j, i: (i, j)),
        out_shape=jax.ShapeDtypeStruct((m, n), F32),
        compiler_params=_cp("parallel", "parallel"),
        name="res_matmul",
    )(*a_list, *b_list, resid, gate, gate)


def _split3(g):
    hi = g.astype(BF16)
    r = g - hi.astype(F32)
    mid = r.astype(BF16)
    lo = (r - mid.astype(F32)).astype(BF16)
    return jnp.concatenate([hi, mid, lo], axis=1)


def _blocks(rows_list):
    return jnp.concatenate([jnp.broadcast_to(r, (GLA_SUB, r.shape[1])) for r in rows_list], axis=0)


def _window_decay_may_be_large(lower_bound_rows, nrows):
    worst = None
    for r0 in range(0, nrows, GLA_SUB):
        part = None
        for r in range(r0, r0 + GLA_SUB, SUBLANE):
            rows = lower_bound_rows(r)
            part = rows if part is None else part + rows
        worst = part if worst is None else jnp.minimum(worst, part)
    return SUBLANE * jnp.min(worst) < -2.0 * SAFE_HALF_DECAY


def _diag_scores_safe(q, k, g, reverse):
    c = q.shape[0]
    kdim = q.shape[1]
    row = lax.broadcasted_iota(jnp.int32, (c, c), 0)
    col = lax.broadcasted_iota(jnp.int32, (c, c), 1)
    prow = lax.broadcasted_iota(jnp.int32, (c, kdim), 0)
    g3 = _split3(g)
    kb = k.astype(BF16)
    scores = jnp.where(row == col, _dot_nt(q.astype(BF16), kb), 0.0)
    w = 1
    while w < GLA_SUB:
        inblock = 2 * w - 1
        mid = (row & ~inblock) + w
        if reverse:
            take = ((col >= row) & (col < mid)) | ((row >= mid) & (col >= mid) & (col < row))
            is_query = (prow & inblock) < w
        else:
            take = ((col >= mid) & (col <= row)) | ((row < mid) & (col > row) & (col < mid))
            is_query = (prow & inblock) >= w
        ee = jnp.dot(jnp.where(take, 1.0, 0.0).astype(BF16), g3, preferred_element_type=F32)
        factor = jnp.exp(ee[:, :kdim] + ee[:, kdim:2 * kdim] + ee[:, 2 * kdim:])
        qm = jnp.where(is_query, q * factor, 0.0).astype(BF16)
        km = jnp.where(is_query, 0.0, k * factor).astype(BF16)
        same = (row & ~inblock) == (col & ~inblock)
        scores = scores + jnp.where(same, _dot_nt(qm, km), 0.0)
        w *= 2
    return scores


def _gla_chunks(qs, ks, vs, gs, sts, reverse, robust, safe_diags):
    c = qs[0].shape[0]
    nsub = c // GLA_SUB
    row = lax.broadcasted_iota(jnp.int32, (c, c), 0)
    col = lax.broadcasted_iota(jnp.int32, (c, c), 1)
    tri = jnp.where((col >= row) if reverse else (col <= row), 1.0, 0.0).astype(BF16)
    srow = lax.broadcasted_iota(jnp.int32, (GLA_SUB, c), 0)
    scol = lax.broadcasted_iota(jnp.int32, (GLA_SUB, c), 1)
    order = list(range(nsub - 1, -1, -1)) if reverse else list(range(nsub))

    bs = []
    for g in gs:
        kdim = g.shape[1]
        bb = jnp.dot(tri, _split3(g), preferred_element_type=F32)
        bs.append(bb[:, :kdim] + bb[:, kdim:2 * kdim] + bb[:, 2 * kdim:])

    work = []
    for q, k, b in zip(qs, ks, bs):
        zero = jnp.zeros((1, q.shape[1]), F32)
        entry, leave = [None] * nsub, [None] * nsub
        for i in range(nsub):
            r0, r1 = i * GLA_SUB, (i + 1) * GLA_SUB
            if reverse:
                entry[i] = b[r1:r1 + 1] if r1 < c else zero
                leave[i] = b[r0:r0 + 1]
            else:
                entry[i] = b[r0 - 1:r0] if r0 > 0 else zero
                leave[i] = b[r1 - 1:r1]
        edge = leave[order[-1]]
        qi = q * jnp.exp(b - _blocks(entry))
        kh = k * jnp.exp(_blocks(leave) - b)
        half = _blocks([jnp.exp(0.5 * (entry[i] - leave[i])) for i in range(nsub)])
        q_in = (qi * _blocks([jnp.exp(e) for e in entry])).astype(BF16)
        k_out = (kh * _blocks([jnp.exp(edge - x) for x in leave])).astype(BF16)
        qd = (qi * half).astype(BF16)
        kd = (kh * half).astype(BF16)
        pairs = []
        for pos, i in enumerate(order):
            for j in order[:pos]:
                r0 = i * GLA_SUB
                pairs.append((i, j, (qi[r0:r0 + GLA_SUB] * jnp.exp(entry[i] - leave[j])).astype(BF16)))
        work.append((q_in, k_out, qd, kd, kh.astype(BF16), pairs, edge))

    atts = []
    for (q_in, k_out, qd, kd, khb, pairs, edge), safe_diag in zip(work, safe_diags):
        diag = jnp.where(robust, safe_diag, _dot_nt(qd, kd))
        off = _dot_nt(jnp.concatenate([p[2] for p in pairs], axis=0), khb)
        blocks = []
        for i in range(nsub):
            r0 = i * GLA_SUB
            inside = (scol >= r0) & (scol < r0 + GLA_SUB)
            keep = inside & ((scol - r0 >= srow) if reverse else (scol - r0 <= srow))
            att = jnp.where(keep, diag[r0:r0 + GLA_SUB], 0.0)
            for n, (pi, pj, _) in enumerate(pairs):
                if pi == i:
                    c0 = pj * GLA_SUB
                    att = jnp.where((scol >= c0) & (scol < c0 + GLA_SUB), off[n * GLA_SUB:(n + 1) * GLA_SUB], att)
            blocks.append(att.astype(BF16))
        atts.append(jnp.concatenate(blocks, axis=0))

    outs, new_sts = [], []
    for (q_in, k_out, qd, kd, khb, pairs, edge), att, v, st in zip(work, atts, vs, sts):
        vb = v.astype(BF16)
        outs.append(jnp.dot(att, vb, preferred_element_type=F32) + _dot_nt(q_in, st.astype(BF16)))
        new_sts.append(st * jnp.exp(edge) + _dot_tn(vb, k_out))
    return outs, new_sts


def _seq_block(t, nblk, reverse):
    if not reverse:
        return t
    n_ctx = 1
    return jnp.where(t < n_ctx, n_ctx - 1 - t, nblk - 1 - (t - n_ctx))


def _hgrn_body(q_ref, z_ref, v_ref, lb_ref, *rest, reverse, finish):
    if finish:
        of_ref, g_ref, gain_ref, y_ref, st_ref, diag_ref = rest
    else:
        o_ref, st_ref, diag_ref = rest

    @pl.when(pl.program_id(2) == 0)
    def _():
        st_ref[...] = jnp.zeros_like(st_ref)
        diag_ref[...] = jnp.zeros_like(diag_ref)

    tb = q_ref.shape[0]
    heads = q_ref.shape[1] // A_DK
    nchunk = tb // GLA_CHUNK
    order = range(nchunk - 1, -1, -1) if reverse else range(nchunk)
    lb = lb_ref[...]
    cols = [slice(h * A_DK, (h + 1) * A_DK) for h in range(heads)]

    def prep(rs):
        qh = _silu(q_ref[rs, :]) * (A_DK ** -0.5)
        sig = jax.nn.sigmoid(z_ref[rs, :])
        key = (1.0 - lb) * (1.0 - sig)
        logf = jnp.log(lb + (1.0 - lb) * sig)
        return qh, key, logf

    robust = _window_decay_may_be_large(
        lambda r: jnp.minimum(z_ref[r:r + SUBLANE, :], 0.0) - LOG2, tb)

    @pl.when(robust)
    def _():
        for c in order:
            rs = slice(c * GLA_CHUNK, (c + 1) * GLA_CHUNK)
            qh, key, logf = prep(rs)
            for h, cs in enumerate(cols):
                diag_ref[c * heads + h] = _diag_scores_safe(qh[:, cs], key[:, cs], logf[:, cs], reverse)

    sts = [st_ref[h] for h in range(heads)]
    for c in order:
        rs = slice(c * GLA_CHUNK, (c + 1) * GLA_CHUNK)
        qh, key, logf = prep(rs)
        v = v_ref[rs, :]
        outs, sts = _gla_chunks([qh[:, cs] for cs in cols], [key[:, cs] for cs in cols], [v[:, cs] for cs in cols],
                                [logf[:, cs] for cs in cols], sts, reverse, robust,
                                [diag_ref[c * heads + h] for h in range(heads)])
        for cs, o in zip(cols, outs):
            if finish:
                o = o + of_ref[rs, cs]
                y = o * lax.rsqrt(jnp.mean(o * o, axis=-1, keepdims=True) + EPS)
                y_ref[rs, cs] = (y * gain_ref[:, cs] * _silu(g_ref[rs, cs])).astype(y_ref.dtype)
            else:
                o_ref[rs, cs] = o
    for h in range(heads):
        st_ref[h] = sts[h]


def _hgrn(proj, lb, nb, nblk, reverse, o_fwd=None, gain=None):
    m = proj.shape[0]
    width = lb.shape[1]
    hw = A_HEADS_PER_STEP * A_DK
    nhb = width // hw
    tb = TOKEN_BLOCK
    finish = o_fwd is not None

    def rows(b, hb, t):
        return b * nblk + _seq_block(t, nblk, reverse)

    def spec(section):
        return pl.BlockSpec((tb, hw), lambda b, hb, t: (rows(b, hb, t), section * nhb + hb))

    in_specs = [spec(0), spec(2 if reverse else 1), spec(3), pl.BlockSpec((1, hw), lambda b, hb, t: (0, hb))]
    args = [proj, proj, proj, lb]
    if finish:
        in_specs += [pl.BlockSpec((tb, hw), lambda b, hb, t: (rows(b, hb, t), hb)), spec(4),
                     pl.BlockSpec((1, hw), lambda b, hb, t: (0, hb))]
        args += [o_fwd, proj, gain]
        out_dtype = BF16
    else:
        out_dtype = F32
    return pl.pallas_call(
        functools.partial(_hgrn_body, reverse=reverse, finish=finish),
        grid=(nb, nhb, nblk),
        in_specs=in_specs,
        out_specs=pl.BlockSpec((tb, hw), lambda b, hb, t: (rows(b, hb, t), hb)),
        out_shape=jax.ShapeDtypeStruct((m, width), out_dtype),
        scratch_shapes=[pltpu.VMEM((A_HEADS_PER_STEP, A_DK, A_DK), F32),
                        pltpu.VMEM((A_HEADS_PER_STEP * (tb // GLA_CHUNK), GLA_CHUNK, GLA_CHUNK), F32)],
        compiler_params=_cp("parallel", "parallel", "arbitrary"),
        name="hgrn_bwd" if reverse else "hgrn_fwd",
    )(*args)


def _log_sigmoid(v):
    return jnp.minimum(v, 0.0) - jnp.log(1.0 + jnp.exp(-jnp.abs(v)))


def _col_specs(c0, width, rows_block, row_map):
    bw = next(w for w in (width, 1536, 1024, 768, 512, 256, LANE) if c0 % w == 0 and width % w == 0)
    return [pl.BlockSpec((rows_block, bw), functools.partial(lambda i, *g: (row_map(*g), c0 // bw + i), i))
            for i in range(width // bw)]


def _cat(refs, rs):
    parts = [r[rs, :] for r in refs]
    return parts[0] if len(parts) == 1 else jnp.concatenate(parts, axis=1)


def _glac_body(*refs, counts, dk, reverse, finish, q_scale):
    nq, nk, nv, ng = counts
    q_refs, refs = refs[:nq], refs[nq:]
    k_refs, refs = refs[:nk], refs[nk:]
    v_refs, refs = refs[:nv], refs[nv:]
    a_ref, wa_ref, ba_ref, wb_ref, fb_ref = refs[:5]
    refs = refs[5:]
    if finish:
        of_ref, refs = refs[0], refs[1:]
        g_refs, refs = refs[:ng], refs[ng:]
        gain_ref, y_ref, st_ref, diag_ref = refs
    else:
        o_ref, st_ref, diag_ref = refs

    @pl.when(pl.program_id(1) == 0)
    def _():
        st_ref[...] = jnp.zeros_like(st_ref)
        diag_ref[...] = jnp.zeros_like(diag_ref)

    tb = a_ref.shape[0]
    kp = st_ref.shape[2]
    vw = st_ref.shape[1]
    nchunk = tb // GLA_CHUNK
    order = range(nchunk - 1, -1, -1) if reverse else range(nchunk)
    a = a_ref[...]
    reach = jnp.dot(jnp.abs(a).astype(BF16), wb_ref[...], preferred_element_type=F32) + fb_ref[...]
    floor = reach * (-1.0 / GLA_NORMALIZER)
    robust = _window_decay_may_be_large(lambda r: floor[r:r + SUBLANE], tb)
    a_hi = a.astype(BF16)
    a_lo = (a - a_hi.astype(F32)).astype(BF16)
    logits = jnp.dot(jnp.concatenate([a_hi, a_lo, a_hi], axis=1), wa_ref[...],
                     preferred_element_type=F32) + ba_ref[...]
    starts = [(h * dk // LANE) * LANE for h in range(C_HEADS)]
    wins = [slice(s, s + kp) for s in starts]
    lane = lax.broadcasted_iota(jnp.int32, (1, kp), 1)
    valid = [(lane >= h * dk - s) & (lane < (h + 1) * dk - s) for h, s in enumerate(starts)]
    vcols = [slice(h * vw, (h + 1) * vw) for h in range(C_HEADS)]

    def operands(rs):
        q = _cat(q_refs, rs) * q_scale
        k = _cat(k_refs, rs)
        gate = _log_sigmoid(logits[rs]) * (1.0 / GLA_NORMALIZER)
        qs = [jnp.where(ok, q[:, w], 0.0) for ok, w in zip(valid, wins)]
        ks = [jnp.where(ok, k[:, w], 0.0) for ok, w in zip(valid, wins)]
        return qs, ks, [gate[:, w] for w in wins]

    @pl.when(robust)
    def _():
        for c in order:
            qs, ks, gs = operands(slice(c * GLA_CHUNK, (c + 1) * GLA_CHUNK))
            for h in range(C_HEADS):
                diag_ref[c * C_HEADS + h] = _diag_scores_safe(qs[h], ks[h], gs[h], reverse)

    states = [st_ref[h] for h in range(C_HEADS)]
    for c in order:
        rs = slice(c * GLA_CHUNK, (c + 1) * GLA_CHUNK)
        qs, ks, gs = operands(rs)
        v = _cat(v_refs, rs)
        outs, states = _gla_chunks(qs, ks, [v[:, vs] for vs in vcols], gs, states, reverse,
                                   robust, [diag_ref[c * C_HEADS + h] for h in range(C_HEADS)])
        if finish:
            g = _cat(g_refs, rs)
        for vs, o in zip(vcols, outs):
            if finish:
                o = o + of_ref[rs, vs]
                y = o * lax.rsqrt(jnp.mean(o * o, axis=-1, keepdims=True) + EPS)
                y_ref[rs, vs] = (y * gain_ref[:, vs] * _silu(g[:, vs])).astype(y_ref.dtype)
            else:
                o_ref[rs, vs] = o
    for h in range(C_HEADS):
        st_ref[h] = states[h]


def _glac(proj, cols, gate_params, dk, vw, nb, nblk, reverse, o_fwd=None, gain=None):
    wa_st, ba, w_bound, bias_bound = gate_params
    m = proj.shape[0]
    kw = ba.shape[1]
    kp = max(-(-((h * dk) % LANE + dk) // LANE) * LANE for h in range(C_HEADS))
    tb = TOKEN_BLOCK
    finish = o_fwd is not None
    q_c, k_c, v_c, g_c, a_c = cols

    def rows(b, t):
        return b * nblk + _seq_block(t, nblk, reverse)

    def const(shape):
        return pl.BlockSpec(shape, lambda b, t: (0, 0))

    q_specs, k_specs, v_specs = (_col_specs(c0, w, tb, rows) for c0, w in ((q_c, kw), (k_c, kw), (v_c, vw)))
    g_specs = _col_specs(g_c, vw, tb, rows) if finish else []
    in_specs = q_specs + k_specs + v_specs + _col_specs(a_c, LANE, tb, rows)
    in_specs += [const((3 * LANE, kw)), const((1, kw)), const((LANE, LANE)), const((1, LANE))]
    args = [proj] * (len(q_specs) + len(k_specs) + len(v_specs) + 1) + [wa_st, ba, w_bound, bias_bound]
    if finish:
        in_specs += [pl.BlockSpec((tb, vw), lambda b, t: (rows(b, t), 0))] + g_specs + [const((1, vw))]
        args += [o_fwd] + [proj] * len(g_specs) + [gain]
    return pl.pallas_call(
        functools.partial(_glac_body, counts=(len(q_specs), len(k_specs), len(v_specs), len(g_specs)), dk=dk,
                          reverse=reverse, finish=finish, q_scale=dk ** -0.5),
        grid=(nb, nblk),
        in_specs=in_specs,
        out_specs=pl.BlockSpec((tb, vw), lambda b, t: (rows(b, t), 0)),
        out_shape=jax.ShapeDtypeStruct((m, vw), BF16 if finish else F32),
        scratch_shapes=[pltpu.VMEM((C_HEADS, vw // C_HEADS, kp), F32),
                        pltpu.VMEM((C_HEADS * (tb // GLA_CHUNK), GLA_CHUNK, GLA_CHUNK), F32)],
        compiler_params=_cp("parallel", "arbitrary"),
        name="gla_bwd" if reverse else "gla_fwd",
    )(*args)


def _lru_body(x_ref, cw_ref, cb_ref, w_ref, bias_ref, lam_ref, o_ref, xc_ref, *, nblk, n_ctx_blk):
    rb = TOKEN_BLOCK
    total = x_ref.shape[0]
    gw = x_ref.shape[1]
    rid = lax.broadcasted_iota(jnp.int32, (SUBLANE, gw), 0)
    cw = cw_ref[...]
    cb = cb_ref[...]

    def conv(blk, start):
        cur = x_ref[pl.ds(start, rb), :]
        seg_start = jnp.logical_or(blk == 0, blk == n_ctx_blk)
        seg_end = jnp.logical_or(blk == n_ctx_blk - 1, blk == nblk - 1)
        pstart = pl.multiple_of(jnp.maximum(start - SUBLANE, 0), SUBLANE)
        nstart = pl.multiple_of(jnp.minimum(start + rb, total - SUBLANE), SUBLANE)
        prev8 = jnp.where(seg_start, 0.0, x_ref[pl.ds(pstart, SUBLANE), :])
        next8 = jnp.where(seg_end, 0.0, x_ref[pl.ds(nstart, SUBLANE), :])
        ext = jnp.concatenate([prev8, cur, next8], axis=0)
        n_ext = rb + 2 * SUBLANE

        def tap(shift):
            return pltpu.roll(ext, shift % n_ext, axis=0)[SUBLANE:SUBLANE + rb]

        return cb + cw[0:1] * tap(2) + cw[1:2] * tap(1) + cw[2:3] * cur + cw[3:4] * tap(-1)

    def run(reverse):
        wsl = slice(2 * gw, 4 * gw) if reverse else slice(0, 2 * gw)
        lam = lam_ref[1:2] if reverse else lam_ref[0:1]
        neg_c_softplus = -RG_C * (jnp.maximum(-lam, 0.0) + jnp.log(1.0 + jnp.exp(-jnp.abs(lam))))
        w = w_ref[:, wsl]
        bias = bias_ref[:, wsl]

        def stage(j):
            blk = _seq_block(j, nblk, reverse)
            start = pl.multiple_of(blk * rb, rb)
            if reverse:
                xc = xc_ref[pl.ds(start, rb), :]
            else:
                xc = conv(blk, start)
                xc_ref[pl.ds(start, rb), :] = xc
            return xc, jnp.dot(xc.astype(BF16), w, preferred_element_type=F32) + bias

        def body(j, carry):
            if reverse:
                h, xc, gates = carry
                xc_next, gates_next = stage(jnp.minimum(j + 1, nblk - 1))
            else:
                h = carry
                xc, gates = stage(j)
            blk = _seq_block(j, nblk, reverse)
            start = pl.multiple_of(blk * rb, rb)
            r = jax.nn.sigmoid(gates[:, :gw])
            i = jax.nn.sigmoid(gates[:, gw:])
            a = jnp.exp(neg_c_softplus * r)
            y = 1.0 - a * a
            u = (y * lax.rsqrt(jnp.maximum(y, TINY))) * (i * xc)
            ngroups = rb // SUBLANE
            order = range(ngroups - 1, -1, -1) if reverse else range(ngroups)
            last = slice(0, 1) if reverse else slice(SUBLANE - 1, SUBLANE)
            scanned = [None] * ngroups
            for gi in order:
                gs = slice(gi * SUBLANE, (gi + 1) * SUBLANE)
                ag, ug = a[gs], u[gs]
                for d in (1, 2, 4):
                    keep = (rid < SUBLANE - d) if reverse else (rid >= d)
                    shift = SUBLANE - d if reverse else d
                    a_s = jnp.where(keep, pltpu.roll(ag, shift, axis=0), 1.0)
                    u_s = jnp.where(keep, pltpu.roll(ug, shift, axis=0), 0.0)
                    ug = ag * u_s + ug
                    ag = ag * a_s
                scanned[gi] = (ag, ug)
            carry = [None] * ngroups
            for gi in order:
                ag, ug = scanned[gi]
                carry[gi] = h
                h = ag[last] * h + ug[last]
            hblk = jnp.concatenate([scanned[gi][1] + scanned[gi][0] * carry[gi] for gi in range(ngroups)], axis=0)
            if reverse:
                o_ref[pl.ds(start, rb), :] = o_ref[pl.ds(start, rb), :] + hblk
            else:
                o_ref[pl.ds(start, rb), :] = hblk
            return (h, xc_next, gates_next) if reverse else h

        h0 = jnp.zeros((1, gw), F32)
        lax.fori_loop(0, nblk, body, (h0,) + stage(0) if reverse else h0)

    run(False)
    run(True)


def _lru(proj, x_off, conv_w, conv_b, w_gates, b_gates, lam, nb, nblk, n_ctx_blk):
    m = proj.shape[0]
    nblocks = w_gates.shape[0]
    total = m // nb
    gw = B_BW
    return pl.pallas_call(
        functools.partial(_lru_body, nblk=nblk, n_ctx_blk=n_ctx_blk),
        grid=(nb, nblocks),
        in_specs=[
            pl.BlockSpec((total, gw), lambda b, n: (b, x_off + n)),
            pl.BlockSpec((B_CONV, gw), lambda b, n: (0, n)),
            pl.BlockSpec((1, gw), lambda b, n: (0, n)),
            pl.BlockSpec((None, gw, 4 * gw), lambda b, n: (n, 0, 0)),
            pl.BlockSpec((None, 1, 4 * gw), lambda b, n: (n, 0, 0)),
            pl.BlockSpec((2, gw), lambda b, n: (0, n)),
        ],
        out_specs=pl.BlockSpec((total, gw), lambda b, n: (b, n)),
        out_shape=jax.ShapeDtypeStruct((m, nblocks * gw), F32),
        scratch_shapes=[pltpu.VMEM((total, gw), F32)],
        compiler_params=_cp("parallel", "parallel"),
        name="rglru_scan",
    )(proj, conv_w, conv_b, w_gates, b_gates, lam)


def _gelu_tanh(v):
    return 0.5 * v * (1.0 + jnp.tanh(0.7978845608028654 * (v + 0.044715 * (v * v * v))))


def _lru_fin_body(h_ref, *refs):
    gate_refs, (gain_ref, y_ref) = refs[:-2], refs[-2:]
    h = h_ref[...]
    y = h * lax.rsqrt(jnp.mean(h * h, axis=-1, keepdims=True) + EPS) * gain_ref[...]
    y_ref[...] = (y * _gelu_tanh(_cat(gate_refs, slice(None)))).astype(y_ref.dtype)


def _lru_fin(h, proj, gate_col, gain):
    m, w = h.shape
    tr = TOKEN_BLOCK
    gate_specs = _col_specs(gate_col, w, tr, lambda i: i)
    return pl.pallas_call(
        _lru_fin_body,
        grid=(m // tr,),
        in_specs=[pl.BlockSpec((tr, w), lambda i: (i, 0))] + gate_specs + [pl.BlockSpec((1, w), lambda i: (0, 0))],
        out_specs=pl.BlockSpec((tr, w), lambda i: (i, 0)),
        out_shape=jax.ShapeDtypeStruct((m, w), BF16),
        compiler_params=_cp("parallel"),
        name="rglru_finish",
    )(h, *([proj] * len(gate_specs)), gain)


def _conv_rows(up, mid, down, w, bias, u, width):
    n, tc = mid.shape
    left = w[3:4] * mid
    cent = w[4:5] * mid
    right = w[5:6] * mid
    if up is not None:
        left += w[0:1] * up
        cent += w[1:2] * up
        right += w[2:3] * up
    if down is not None:
        left += w[6:7] * down
        cent += w[7:8] * down
        right += w[8:9] * down
    cidx = lax.broadcasted_iota(jnp.int32, (n, tc), 0) & (width - 1)
    acc = cent + bias
    acc += jnp.where(cidx != 0, pltpu.roll(left, 1, axis=0), 0.0)
    acc += jnp.where(cidx != width - 1, pltpu.roll(right, n - 1, axis=0), 0.0)
    return (_silu(acc) * u).astype(BF16)


def _ffn_mid_body(u_ref, g_ref, w_ref, b_ref, o_ref, *, ctx_rows):
    total = u_ref.shape[0]
    w = w_ref[...]
    bias = b_ref[...]
    o_ref[0:ctx_rows, :] = _conv_rows(None, g_ref[0:ctx_rows, :], None, w, bias, u_ref[0:ctx_rows, :], ctx_rows)
    step = FFN_ROWS
    nsteps = (total - ctx_rows) // step
    zeros = jnp.zeros((GRID_W, u_ref.shape[1]), F32)

    def load(start, n):
        return g_ref[pl.ds(start, n), :]

    def emit(start, up, mid, down):
        o_ref[pl.ds(start, step), :] = _conv_rows(up, mid, down, w, bias, u_ref[pl.ds(start, step), :], GRID_W)

    first = ctx_rows
    last = ctx_rows + (nsteps - 1) * step
    if nsteps == 1:
        mid = load(first, step)
        emit(first, jnp.concatenate([zeros, mid[:step - GRID_W]], axis=0), mid,
             jnp.concatenate([mid[GRID_W:], zeros], axis=0))
        return
    mid = load(first, step)
    emit(first, jnp.concatenate([zeros, mid[:step - GRID_W]], axis=0), mid, load(first + GRID_W, step))

    def body(s, carry):
        start = pl.multiple_of(ctx_rows + s * step, GRID_W)
        emit(start, load(start - GRID_W, step), load(start, step), load(start + GRID_W, step))
        return carry

    lax.fori_loop(1, nsteps - 1, body, 0)
    mid = load(last, step)
    emit(last, load(last - GRID_W, step), mid, jnp.concatenate([mid[GRID_W:], zeros], axis=0))


def _ffn_mid(up, conv_w9, conv_b, nb, ctx_rows, tc=LANE):
    m, two_f = up.shape
    f = two_f // 2
    total = m // nb
    nct = f // tc
    return pl.pallas_call(
        functools.partial(_ffn_mid_body, ctx_rows=ctx_rows),
        grid=(nb, nct),
        in_specs=[
            pl.BlockSpec((total, tc), lambda b, j: (b, j)),
            pl.BlockSpec((total, tc), lambda b, j: (b, nct + j)),
            pl.BlockSpec((9, tc), lambda b, j: (0, j)),
            pl.BlockSpec((1, tc), lambda b, j: (0, j)),
        ],
        out_specs=pl.BlockSpec((total, tc), lambda b, j: (b, j)),
        out_shape=jax.ShapeDtypeStruct((m, f), BF16),
        compiler_params=_cp("parallel", "parallel"),
        name="ffn_conv_gate",
    )(up, up, conv_w9, conv_b)


def _final_norm_body(x_ref, g_ref, o_ref):
    x = x_ref[...]
    o_ref[...] = x * lax.rsqrt(jnp.mean(x * x, axis=-1, keepdims=True) + EPS) * g_ref[...]


def _final_norm(x, g, nb, nblk, n_ctx_blk, seq):
    d = x.shape[1]
    tr = TOKEN_BLOCK
    return pl.pallas_call(
        _final_norm_body,
        grid=(nb, seq // tr),
        in_specs=[
            pl.BlockSpec((tr, d), lambda b, i: (b * nblk + n_ctx_blk + i, 0)),
            pl.BlockSpec((1, d), lambda b, i: (0, 0)),
        ],
        out_specs=pl.BlockSpec((None, tr, d), lambda b, i: (b, i, 0)),
        out_shape=jax.ShapeDtypeStruct((nb, seq, d), F32),
        compiler_params=_cp("parallel", "parallel"),
        name="final_norm",
    )(x, g.reshape(1, d))


def _lower_bounds_body(x_ref, o_ref):
    depth = x_ref.shape[0]
    rows = [x_ref[i:i + 1, :] for i in range(depth)]
    top = functools.reduce(jnp.maximum, rows)
    es = [jnp.exp(r - top) for r in rows]
    total = functools.reduce(lambda a, b: a + b, es)
    ps = [e / total for e in es]
    run = None
    for i, p in enumerate(ps):
        run = p if run is None else run + p
        o_ref[i:i + 1, :] = jnp.clip(run - ps[0], 0.0, 1.0)


def _lower_bounds(logits):
    return pl.pallas_call(
        _lower_bounds_body,
        out_shape=jax.ShapeDtypeStruct(logits.shape, F32),
        name="hgrn_lower_bounds",
    )(logits.astype(F32))


def _pick_tile(total, candidates):
    for t in candidates:
        if total % t == 0:
            return t
    raise ValueError(f"no tile for {total}")


def kernel(x, c, ctx, c_ctx, w_ada, b_ada, norm1_g, w_in, hgrn_lb_fwd, hgrn_lb_bwd, hgrn_norm_g, lru_conv_w, lru_conv_b, lru_wr_fwd, lru_br_fwd, lru_wi_fwd, lru_bi_fwd, lru_lam_fwd, lru_wr_bwd, lru_br_bwd, lru_wi_bwd, lru_bi_bwd, lru_lam_bwd, lru_norm_g, gla_wa_fwd, gla_ba_fwd, gla_wa_bwd, gla_ba_bwd, gla_norm_g, w_out, norm2_g, w_up, ffn_conv_w, ffn_conv_b, w_down, final_norm_g):
    nb, seq, d = x.shape
    ctx_len = ctx.shape[1]
    depth = w_ada.shape[0]
    assert ctx_len == TOKEN_BLOCK and seq % FFN_ROWS == 0
    rows_per_batch = ctx_len + seq
    nblk = rows_per_batch // TOKEN_BLOCK
    n_ctx_blk = ctx_len // TOKEN_BLOCK
    m = nb * rows_per_batch

    a_w = hgrn_norm_g.shape[1]
    b_w = lru_norm_g.shape[1]
    c_w = gla_norm_g.shape[1]
    c_k = gla_ba_fwd.shape[1]
    c_dk = c_k // C_HEADS
    n_blocks = b_w // B_BW

    col_bx = 5 * a_w
    col_bg = col_bx + b_w
    col_cq = col_bg + b_w
    col_ck = col_cq + c_k
    col_cv = col_ck + c_k
    col_cg = col_cv + c_w
    col_ab = col_cg + c_w
    assert col_ab % LANE == 0 and col_ab + 2 * C_RANK == w_in.shape[2]
    tn_in = 512
    n_proj = -(-w_in.shape[2] // tn_in) * tn_in

    tm = _pick_tile(rows_per_batch, (768, 512, 256))
    tm_in = _pick_tile(rows_per_batch, (1408, 768, 512, 256))

    xs = jnp.concatenate([ctx, x], axis=1).reshape(m, d)

    c16 = jnp.zeros((16, d), F32).at[:nb].set(c).at[nb].set(c_ctx)
    mods = _ada(c16, w_ada, b_ada)
    lbf_all = _lower_bounds(hgrn_lb_fwd)
    lbb_all = _lower_bounds(hgrn_lb_bwd)

    w_in_b = _cast_pad(jnp.swapaxes(w_in, 1, 2), n_proj)
    w_up_b = w_up.astype(BF16)
    w_down_b = w_down.astype(BF16)
    wo_parts = [w_out[:, :a_w].astype(BF16), w_out[:, a_w:a_w + b_w].astype(BF16), w_out[:, a_w + b_w:].astype(BF16)]

    def gate_params(wa, ba, row0):
        wp = jnp.zeros((LANE, c_k), F32).at[row0:row0 + C_RANK].set(wa)
        hi = wp.astype(BF16)
        lo = (wp - hi.astype(F32)).astype(BF16)
        w_bound = jnp.broadcast_to(jnp.max(jnp.abs(wp), axis=1, keepdims=True) * BOUND_MARGIN, (LANE, LANE))
        bias_bound = jnp.full((1, LANE), LOG2, F32) + jnp.max(jnp.abs(ba))
        return jnp.concatenate([hi, hi, lo], axis=0), ba.reshape(1, c_k), w_bound.astype(BF16), bias_bound

    for l in range(depth):
        mod = mods[l, :nb + 1].reshape(nb + 1, 6, 1, d)
        sh1, sc1, g1, sh2, sc2, g2 = (mod[:, i] for i in range(6))

        h = _norm_mod(xs, norm1_g[l], sh1, sc1, nblk, n_ctx_blk)
        proj = _matmul(h, w_in_b, l, tm_in, tn_in, a_resident=True, b_transposed=True)

        lbf = lbf_all[l].reshape(1, a_w)
        lbb = lbb_all[l].reshape(1, a_w)
        oa = _hgrn(proj, lbf, nb, nblk, reverse=False)
        ya = _hgrn(proj, lbb, nb, nblk, reverse=True, o_fwd=oa, gain=hgrn_norm_g[l].reshape(1, a_w))

        w_gates = jnp.concatenate([lru_wr_fwd[l], lru_wi_fwd[l], lru_wr_bwd[l], lru_wi_bwd[l]], axis=-1).astype(BF16)
        b_gates = jnp.concatenate([t[l].reshape(n_blocks, 1, B_BW) for t in
                                   (lru_br_fwd, lru_bi_fwd, lru_br_bwd, lru_bi_bwd)], axis=-1)
        lam = jnp.stack([lru_lam_fwd[l], lru_lam_bwd[l]], axis=0)
        hb = _lru(proj, col_bx // LANE, lru_conv_w[l], lru_conv_b[l].reshape(1, b_w), w_gates, b_gates, lam,
                  nb, nblk, n_ctx_blk)
        yb = _lru_fin(hb, proj, col_bg, lru_norm_g[l].reshape(1, b_w))

        cols = (col_cq, col_ck, col_cv, col_cg, col_ab)
        gain_c = gla_norm_g[l].reshape(1, c_w)
        oc = _glac(proj, cols, gate_params(gla_wa_fwd[l], gla_ba_fwd[l], 0), c_dk, c_w, nb, nblk, reverse=False)
        yc = _glac(proj, cols, gate_params(gla_wa_bwd[l], gla_ba_bwd[l], C_RANK), c_dk, c_w, nb, nblk,
                   reverse=True, o_fwd=oc, gain=gain_c)

        xs = _matmul_res([ya, yb, yc], wo_parts, l, xs, g1, tm, 1024, rows_per_batch, ctx_len)

        h2 = _norm_mod(xs, norm2_g[l], sh2, sc2, nblk, n_ctx_blk)
        up = _matmul(h2, w_up_b, l, tm, 1024)
        act = _ffn_mid(up, ffn_conv_w[l].reshape(9, -1), ffn_conv_b[l].reshape(1, -1), nb, ctx_len)
        xs = _matmul_res([act], [w_down_b], l, xs, g2, tm, 1024, rows_per_batch, ctx_len)

    return _final_norm(xs, final_norm_g, nb, nblk, n_ctx_blk, seq)
```

```python
import functools

import jax
import jax.numpy as jnp
from jax import lax
from jax.experimental import pallas as pl
from jax.experimental.pallas import tpu as pltpu

F32 = jnp.float32
BF16 = jnp.bfloat16

EPS = 1e-6
GRID_W = 64
LANE = 128
SUBLANE = 8
A_DK = 128
A_HEADS_PER_STEP = 12
B_BW = 128
B_CONV = 4
LRU_TILES_PER_STEP = 2
RG_C = 8.0
C_HEADS = 4
C_RANK = 16
GLA_NORMALIZER = 16.0
TOKEN_BLOCK = 256
GLA_CHUNK = 128
GLA_SUB = 32
LOG2 = 0.6931471805599453
TINY = 1e-30
BOUND_MARGIN = 1.02
SAFE_HALF_DECAY = 60.0
FFN_ROWS = 512
V7X_VMEM_BYTES = 64 * 1024 * 1024
VMEM_LIMIT = V7X_VMEM_BYTES * 7 // 8


def _cp(*sem):
    return pltpu.CompilerParams(dimension_semantics=sem, vmem_limit_bytes=VMEM_LIMIT)


def _dot_nt(a, b):
    return lax.dot_general(a, b, (((1,), (1,)), ((), ())), preferred_element_type=F32)


def _dot_tn(a, b):
    return lax.dot_general(a, b, (((0,), (0,)), ((), ())), preferred_element_type=F32)


def _silu(v):
    return v * jax.nn.sigmoid(v)


def _ada_body(c_ref, w_ref, b_ref, o_ref):
    a = _silu(c_ref[...]).astype(BF16)
    o_ref[...] = jnp.dot(a, w_ref[...].astype(BF16), preferred_element_type=F32) + b_ref[...]


def _ada(c16, w_ada, b_ada, tn=512):
    depth, d, n = w_ada.shape
    rows = c16.shape[0]
    return pl.pallas_call(
        _ada_body,
        grid=(depth, n // tn),
        in_specs=[
            pl.BlockSpec((rows, d), lambda l, j: (0, 0)),
            pl.BlockSpec((None, d, tn), lambda l, j: (l, 0, j)),
            pl.BlockSpec((None, 1, tn), lambda l, j: (l, 0, j)),
        ],
        out_specs=pl.BlockSpec((None, rows, tn), lambda l, j: (l, 0, j)),
        out_shape=jax.ShapeDtypeStruct((depth, rows, n), F32),
        compiler_params=_cp("parallel", "parallel"),
        name="ada_mod",
    )(c16, w_ada, b_ada.reshape(depth, 1, n))


def _norm_mod_body(x_ref, g_ref, sh_ref, sc_ref, o_ref):
    x = x_ref[...]
    y = x * lax.rsqrt(jnp.mean(x * x, axis=-1, keepdims=True) + EPS) * g_ref[...]
    o_ref[...] = (y * (1.0 + sc_ref[...]) + sh_ref[...]).astype(o_ref.dtype)


def _norm_mod(x, g, shift, scale, nblk, n_ctx_blk):
    m, d = x.shape
    tr = TOKEN_BLOCK
    nb = shift.shape[0] - 1

    def mod_row(i):
        return jnp.where(i % nblk < n_ctx_blk, nb, i // nblk)

    return pl.pallas_call(
        _norm_mod_body,
        grid=(m // tr,),
        in_specs=[
            pl.BlockSpec((tr, d), lambda i: (i, 0)),
            pl.BlockSpec((1, d), lambda i: (0, 0)),
            pl.BlockSpec((None, 1, d), lambda i: (mod_row(i), 0, 0)),
            pl.BlockSpec((None, 1, d), lambda i: (mod_row(i), 0, 0)),
        ],
        out_specs=pl.BlockSpec((tr, d), lambda i: (i, 0)),
        out_shape=jax.ShapeDtypeStruct((m, d), BF16),
        compiler_params=_cp("parallel"),
        name="norm_mod",
    )(x, g.reshape(1, d), shift, scale)


def _cast_pad_body(w_ref, o_ref, *, n_valid):
    row = lax.broadcasted_iota(jnp.int32, w_ref.shape, 0) + pl.program_id(1) * w_ref.shape[0]
    o_ref[...] = jnp.where(row < n_valid, w_ref[...], 0.0).astype(o_ref.dtype)


def _cast_pad(wt, n_out, tn=512):
    depth, n, k = wt.shape
    return pl.pallas_call(
        functools.partial(_cast_pad_body, n_valid=n),
        grid=(depth, n_out // tn),
        in_specs=[pl.BlockSpec((None, tn, k), lambda l, j: (l, j, 0))],
        out_specs=pl.BlockSpec((None, tn, k), lambda l, j: (l, j, 0)),
        out_shape=jax.ShapeDtypeStruct((depth, n_out, k), BF16),
        compiler_params=_cp("parallel", "parallel"),
        name="cast_pad_weights",
    )(wt)


def _mm_body(a_ref, b_ref, o_ref, *, b_transposed):
    if b_transposed:
        acc = _dot_nt(a_ref[...], b_ref[...])
    else:
        acc = jnp.dot(a_ref[...], b_ref[...], preferred_element_type=F32)
    o_ref[...] = acc.astype(o_ref.dtype)


def _matmul(a, b, layer, tm, tn, out_dtype=F32, a_resident=False, b_transposed=False):
    m, k = a.shape
    n = b.shape[1] if b_transposed else b.shape[2]
    b_block = (None, tn, k) if b_transposed else (None, k, tn)
    if a_resident:
        grid = (m // tm, n // tn)
        a_map, o_map = (lambda i, j: (i, 0)), (lambda i, j: (i, j))
        b_map = (lambda i, j: (layer, j, 0)) if b_transposed else (lambda i, j: (layer, 0, j))
    else:
        grid = (n // tn, m // tm)
        a_map, o_map = (lambda j, i: (i, 0)), (lambda j, i: (i, j))
        b_map = (lambda j, i: (layer, j, 0)) if b_transposed else (lambda j, i: (layer, 0, j))
    return pl.pallas_call(
        functools.partial(_mm_body, b_transposed=b_transposed),
        grid=grid,
        in_specs=[pl.BlockSpec((tm, k), a_map), pl.BlockSpec(b_block, b_map)],
        out_specs=pl.BlockSpec((tm, tn), o_map),
        out_shape=jax.ShapeDtypeStruct((m, n), out_dtype),
        compiler_params=_cp("parallel", "parallel"),
        name="proj_matmul",
    )(a, b)


def _mm_res_body(*refs, n_ops, ctx_rows, tiles_per_batch):
    a_refs = refs[:n_ops]
    b_refs = refs[n_ops:2 * n_ops]
    r_ref, gb_ref, gc_ref, o_ref = refs[2 * n_ops:]
    acc = jnp.dot(a_refs[0][...], b_refs[0][...], preferred_element_type=F32)
    for a_ref, b_ref in zip(a_refs[1:], b_refs[1:]):
        acc += jnp.dot(a_ref[...], b_ref[...], preferred_element_type=F32)
    tm = acc.shape[0]
    first = pl.program_id(1) % tiles_per_batch == 0
    row = lax.broadcasted_iota(jnp.int32, (tm, 1), 0)
    is_ctx = jnp.logical_and(first, row < ctx_rows)
    gate = jnp.where(is_ctx, gc_ref[...], gb_ref[...])
    o_ref[...] = r_ref[...] + gate * acc


def _matmul_res(a_list, b_list, layer, resid, gate, tm, tn, rows_per_batch, ctx_rows):
    m, n = resid.shape
    nb = gate.shape[0] - 1
    tiles_per_batch = rows_per_batch // tm
    assert ctx_rows <= tm
    n_ops = len(a_list)
    in_specs = [pl.BlockSpec((tm, a.shape[1]), lambda j, i: (i, 0)) for a in a_list]
    in_specs += [pl.BlockSpec((None, b.shape[1], tn), lambda j, i: (layer, 0, j)) for b in b_list]
    in_specs += [
        pl.BlockSpec((tm, tn), lambda j, i: (i, j)),
        pl.BlockSpec((None, 1, tn), lambda j, i: (i // tiles_per_batch, 0, j)),
        pl.BlockSpec((None, 1, tn), lambda j, i: (nb, 0, j)),
    ]
    return pl.pallas_call(
        functools.partial(_mm_res_body, n_ops=n_ops, ctx_rows=ctx_rows, tiles_per_batch=tiles_per_batch),
        grid=(n // tn, m // tm),
        in_specs=in_specs,
        out_specs=pl.BlockSpec((tm, tn), lambda j, i: (i, j)),
        out_shape=jax.ShapeDtypeStruct((m, n), F32),
        compiler_params=_cp("parallel", "parallel"),
        name="res_matmul",
    )(*a_list, *b_list, resid, gate, gate)


def _split3(g):
    hi = g.astype(BF16)
    r = g - hi.astype(F32)
    mid = r.astype(BF16)
    lo = (r - mid.astype(F32)).astype(BF16)
    return jnp.concatenate([hi, mid, lo], axis=1)


def _blocks(rows_list):
    return jnp.concatenate([jnp.broadcast_to(r, (GLA_SUB, r.shape[1])) for r in rows_list], axis=0)


def _window_decay_may_be_large(lower_bound_rows, nrows):
    worst = None
    for r0 in range(0, nrows, GLA_SUB):
        part = None
        for r in range(r0, r0 + GLA_SUB, SUBLANE):
            rows = lower_bound_rows(r)
            part = rows if part is None else part + rows
        worst = part if worst is None else jnp.minimum(worst, part)
    return SUBLANE * jnp.min(worst) < -2.0 * SAFE_HALF_DECAY


def _diag_scores_safe(q, k, g, reverse):
    c = q.shape[0]
    kdim = q.shape[1]
    row = lax.broadcasted_iota(jnp.int32, (c, c), 0)
    col = lax.broadcasted_iota(jnp.int32, (c, c), 1)
    prow = lax.broadcasted_iota(jnp.int32, (c, kdim), 0)
    g3 = _split3(g)
    kb = k.astype(BF16)
    scores = jnp.where(row == col, _dot_nt(q.astype(BF16), kb), 0.0)
    w = 1
    while w < GLA_SUB:
        inblock = 2 * w - 1
        mid = (row & ~inblock) + w
        if reverse:
            take = ((col >= row) & (col < mid)) | ((row >= mid) & (col >= mid) & (col < row))
            is_query = (prow & inblock) < w
        else:
            take = ((col >= mid) & (col <= row)) | ((row < mid) & (col > row) & (col < mid))
            is_query = (prow & inblock) >= w
        ee = jnp.dot(jnp.where(take, 1.0, 0.0).astype(BF16), g3, preferred_element_type=F32)
        factor = jnp.exp(ee[:, :kdim] + ee[:, kdim:2 * kdim] + ee[:, 2 * kdim:])
        qm = jnp.where(is_query, q * factor, 0.0).astype(BF16)
        km = jnp.where(is_query, 0.0, k * factor).astype(BF16)
        same = (row & ~inblock) == (col & ~inblock)
        scores = scores + jnp.where(same, _dot_nt(qm, km), 0.0)
        w *= 2
    return scores


def _gla_chunks(qs, ks, vs, gs, sts, reverse, robust, safe_diags):
    c = qs[0].shape[0]
    nsub = c // GLA_SUB
    row = lax.broadcasted_iota(jnp.int32, (c, c), 0)
    col = lax.broadcasted_iota(jnp.int32, (c, c), 1)
    tri = jnp.where((col >= row) if reverse else (col <= row), 1.0, 0.0).astype(BF16)
    srow = lax.broadcasted_iota(jnp.int32, (GLA_SUB, c), 0)
    scol = lax.broadcasted_iota(jnp.int32, (GLA_SUB, c), 1)
    order = list(range(nsub - 1, -1, -1)) if reverse else list(range(nsub))

    bs = []
    for g in gs:
        kdim = g.shape[1]
        bb = jnp.dot(tri, _split3(g), preferred_element_type=F32)
        bs.append(bb[:, :kdim] + bb[:, kdim:2 * kdim] + bb[:, 2 * kdim:])

    work = []
    for q, k, b in zip(qs, ks, bs):
        zero = jnp.zeros((1, q.shape[1]), F32)
        entry, leave = [None] * nsub, [None] * nsub
        for i in range(nsub):
            r0, r1 = i * GLA_SUB, (i + 1) * GLA_SUB
            if reverse:
                entry[i] = b[r1:r1 + 1] if r1 < c else zero
                leave[i] = b[r0:r0 + 1]
            else:
                entry[i] = b[r0 - 1:r0] if r0 > 0 else zero
                leave[i] = b[r1 - 1:r1]
        edge = leave[order[-1]]
        qi = q * jnp.exp(b - _blocks(entry))
        kh = k * jnp.exp(_blocks(leave) - b)
        half = _blocks([jnp.exp(0.5 * (entry[i] - leave[i])) for i in range(nsub)])
        q_in = (qi * _blocks([jnp.exp(e) for e in entry])).astype(BF16)
        k_out = (kh * _blocks([jnp.exp(edge - x) for x in leave])).astype(BF16)
        qd = (qi * half).astype(BF16)
        kd = (kh * half).astype(BF16)
        pairs = []
        for pos, i in enumerate(order):
            for j in order[:pos]:
                r0 = i * GLA_SUB
                pairs.append((i, j, (qi[r0:r0 + GLA_SUB] * jnp.exp(entry[i] - leave[j])).astype(BF16)))
        work.append((q_in, k_out, qd, kd, kh.astype(BF16), pairs, edge))

    atts = []
    for (q_in, k_out, qd, kd, khb, pairs, edge), safe_diag in zip(work, safe_diags):
        diag = jnp.where(robust, safe_diag, _dot_nt(qd, kd))
        off = _dot_nt(jnp.concatenate([p[2] for p in pairs], axis=0), khb)
        blocks = []
        for i in range(nsub):
            r0 = i * GLA_SUB
            inside = (scol >= r0) & (scol < r0 + GLA_SUB)
            keep = inside & ((scol - r0 >= srow) if reverse else (scol - r0 <= srow))
            att = jnp.where(keep, diag[r0:r0 + GLA_SUB], 0.0)
            for n, (pi, pj, _) in enumerate(pairs):
                if pi == i:
                    c0 = pj * GLA_SUB
                    att = jnp.where((scol >= c0) & (scol < c0 + GLA_SUB), off[n * GLA_SUB:(n + 1) * GLA_SUB], att)
            blocks.append(att.astype(BF16))
        atts.append(jnp.concatenate(blocks, axis=0))

    outs, new_sts = [], []
    for (q_in, k_out, qd, kd, khb, pairs, edge), att, v, st in zip(work, atts, vs, sts):
        vb = v.astype(BF16)
        outs.append(jnp.dot(att, vb, preferred_element_type=F32) + _dot_nt(q_in, st.astype(BF16)))
        new_sts.append(st * jnp.exp(edge) + _dot_tn(vb, k_out))
    return outs, new_sts


def _seq_block(t, nblk, reverse):
    if not reverse:
        return t
    n_ctx = 1
    return jnp.where(t < n_ctx, n_ctx - 1 - t, nblk - 1 - (t - n_ctx))


def _hgrn_body(q_ref, z_ref, v_ref, lb_ref, *rest, reverse, finish):
    if finish:
        of_ref, g_ref, gain_ref, y_ref, st_ref, diag_ref = rest
    else:
        o_ref, st_ref, diag_ref = rest

    @pl.when(pl.program_id(2) == 0)
    def _():
        st_ref[...] = jnp.zeros_like(st_ref)
        diag_ref[...] = jnp.zeros_like(diag_ref)

    tb = q_ref.shape[0]
    heads = q_ref.shape[1] // A_DK
    nchunk = tb // GLA_CHUNK
    order = range(nchunk - 1, -1, -1) if reverse else range(nchunk)
    lb = lb_ref[...]
    cols = [slice(h * A_DK, (h + 1) * A_DK) for h in range(heads)]

    def prep(rs):
        qh = _silu(q_ref[rs, :]) * (A_DK ** -0.5)
        sig = jax.nn.sigmoid(z_ref[rs, :])
        key = (1.0 - lb) * (1.0 - sig)
        logf = jnp.log(lb + (1.0 - lb) * sig)
        return qh, key, logf

    robust = _window_decay_may_be_large(
        lambda r: jnp.minimum(z_ref[r:r + SUBLANE, :], 0.0) - LOG2, tb)

    @pl.when(robust)
    def _():
        for c in order:
            rs = slice(c * GLA_CHUNK, (c + 1) * GLA_CHUNK)
            qh, key, logf = prep(rs)
            for h, cs in enumerate(cols):
                diag_ref[c * heads + h] = _diag_scores_safe(qh[:, cs], key[:, cs], logf[:, cs], reverse)

    sts = [st_ref[h] for h in range(heads)]
    for c in order:
        rs = slice(c * GLA_CHUNK, (c + 1) * GLA_CHUNK)
        qh, key, logf = prep(rs)
        v = v_ref[rs, :]
        outs, sts = _gla_chunks([qh[:, cs] for cs in cols], [key[:, cs] for cs in cols], [v[:, cs] for cs in cols],
                                [logf[:, cs] for cs in cols], sts, reverse, robust,
                                [diag_ref[c * heads + h] for h in range(heads)])
        for cs, o in zip(cols, outs):
            if finish:
                o = o + of_ref[rs, cs]
                y = o * lax.rsqrt(jnp.mean(o * o, axis=-1, keepdims=True) + EPS)
                y_ref[rs, cs] = (y * gain_ref[:, cs] * _silu(g_ref[rs, cs])).astype(y_ref.dtype)
            else:
                o_ref[rs, cs] = o
    for h in range(heads):
        st_ref[h] = sts[h]


def _hgrn(proj, lb, nb, nblk, reverse, o_fwd=None, gain=None):
    m = proj.shape[0]
    width = lb.shape[1]
    hw = A_HEADS_PER_STEP * A_DK
    nhb = width // hw
    tb = TOKEN_BLOCK
    finish = o_fwd is not None

    def rows(b, hb, t):
        return b * nblk + _seq_block(t, nblk, reverse)

    def spec(section):
        return pl.BlockSpec((tb, hw), lambda b, hb, t: (rows(b, hb, t), section * nhb + hb))

    in_specs = [spec(0), spec(2 if reverse else 1), spec(3), pl.BlockSpec((1, hw), lambda b, hb, t: (0, hb))]
    args = [proj, proj, proj, lb]
    if finish:
        in_specs += [pl.BlockSpec((tb, hw), lambda b, hb, t: (rows(b, hb, t), hb)), spec(4),
                     pl.BlockSpec((1, hw), lambda b, hb, t: (0, hb))]
        args += [o_fwd, proj, gain]
        out_dtype = BF16
    else:
        out_dtype = F32
    return pl.pallas_call(
        functools.partial(_hgrn_body, reverse=reverse, finish=finish),
        grid=(nb, nhb, nblk),
        in_specs=in_specs,
        out_specs=pl.BlockSpec((tb, hw), lambda b, hb, t: (rows(b, hb, t), hb)),
        out_shape=jax.ShapeDtypeStruct((m, width), out_dtype),
        scratch_shapes=[pltpu.VMEM((A_HEADS_PER_STEP, A_DK, A_DK), F32),
                        pltpu.VMEM((A_HEADS_PER_STEP * (tb // GLA_CHUNK), GLA_CHUNK, GLA_CHUNK), F32)],
        compiler_params=_cp("parallel", "parallel", "arbitrary"),
        name="hgrn_bwd" if reverse else "hgrn_fwd",
    )(*args)


def _log_sigmoid(v):
    return jnp.minimum(v, 0.0) - jnp.log(1.0 + jnp.exp(-jnp.abs(v)))


def _col_specs(c0, width, rows_block, row_map):
    bw = next(w for w in (width, 1536, 1024, 768, 512, 256, LANE) if c0 % w == 0 and width % w == 0)
    return [pl.BlockSpec((rows_block, bw), functools.partial(lambda i, *g: (row_map(*g), c0 // bw + i), i))
            for i in range(width // bw)]


def _cat(refs, rs):
    parts = [r[rs, :] for r in refs]
    return parts[0] if len(parts) == 1 else jnp.concatenate(parts, axis=1)


def _glac_body(*refs, counts, dk, reverse, finish, q_scale):
    nq, nk, nv, ng = counts
    q_refs, refs = refs[:nq], refs[nq:]
    k_refs, refs = refs[:nk], refs[nk:]
    v_refs, refs = refs[:nv], refs[nv:]
    a_ref, wa_ref, ba_ref, wb_ref, fb_ref = refs[:5]
    refs = refs[5:]
    if finish:
        of_ref, refs = refs[0], refs[1:]
        g_refs, refs = refs[:ng], refs[ng:]
        gain_ref, y_ref, st_ref, diag_ref = refs
    else:
        o_ref, st_ref, diag_ref = refs

    @pl.when(pl.program_id(1) == 0)
    def _():
        st_ref[...] = jnp.zeros_like(st_ref)
        diag_ref[...] = jnp.zeros_like(diag_ref)

    tb = a_ref.shape[0]
    kp = st_ref.shape[2]
    vw = st_ref.shape[1]
    nchunk = tb // GLA_CHUNK
    order = range(nchunk - 1, -1, -1) if reverse else range(nchunk)
    a = a_ref[...]
    reach = jnp.dot(jnp.abs(a).astype(BF16), wb_ref[...], preferred_element_type=F32) + fb_ref[...]
    floor = reach * (-1.0 / GLA_NORMALIZER)
    robust = _window_decay_may_be_large(lambda r: floor[r:r + SUBLANE], tb)
    a_hi = a.astype(BF16)
    a_lo = (a - a_hi.astype(F32)).astype(BF16)
    logits = jnp.dot(jnp.concatenate([a_hi, a_lo, a_hi], axis=1), wa_ref[...],
                     preferred_element_type=F32) + ba_ref[...]
    starts = [(h * dk // LANE) * LANE for h in range(C_HEADS)]
    wins = [slice(s, s + kp) for s in starts]
    lane = lax.broadcasted_iota(jnp.int32, (1, kp), 1)
    valid = [(lane >= h * dk - s) & (lane < (h + 1) * dk - s) for h, s in enumerate(starts)]
    vcols = [slice(h * vw, (h + 1) * vw) for h in range(C_HEADS)]

    def operands(rs):
        q = _cat(q_refs, rs) * q_scale
        k = _cat(k_refs, rs)
        gate = _log_sigmoid(logits[rs]) * (1.0 / GLA_NORMALIZER)
        qs = [jnp.where(ok, q[:, w], 0.0) for ok, w in zip(valid, wins)]
        ks = [jnp.where(ok, k[:, w], 0.0) for ok, w in zip(valid, wins)]
        return qs, ks, [gate[:, w] for w in wins]

    @pl.when(robust)
    def _():
        for c in order:
            qs, ks, gs = operands(slice(c * GLA_CHUNK, (c + 1) * GLA_CHUNK))
            for h in range(C_HEADS):
                diag_ref[c * C_HEADS + h] = _diag_scores_safe(qs[h], ks[h], gs[h], reverse)

    states = [st_ref[h] for h in range(C_HEADS)]
    for c in order:
        rs = slice(c * GLA_CHUNK, (c + 1) * GLA_CHUNK)
        qs, ks, gs = operands(rs)
        v = _cat(v_refs, rs)
        outs, states = _gla_chunks(qs, ks, [v[:, vs] for vs in vcols], gs, states, reverse,
                                   robust, [diag_ref[c * C_HEADS + h] for h in range(C_HEADS)])
        if finish:
            g = _cat(g_refs, rs)
        for vs, o in zip(vcols, outs):
            if finish:
                o = o + of_ref[rs, vs]
                y = o * lax.rsqrt(jnp.mean(o * o, axis=-1, keepdims=True) + EPS)
                y_ref[rs, vs] = (y * gain_ref[:, vs] * _silu(g[:, vs])).astype(y_ref.dtype)
            else:
                o_ref[rs, vs] = o
    for h in range(C_HEADS):
        st_ref[h] = states[h]


def _glac(proj, cols, gate_params, dk, vw, nb, nblk, reverse, o_fwd=None, gain=None):
    wa_st, ba, w_bound, bias_bound = gate_params
    m = proj.shape[0]
    kw = ba.shape[1]
    kp = max(-(-((h * dk) % LANE + dk) // LANE) * LANE for h in range(C_HEADS))
    tb = TOKEN_BLOCK
    finish = o_fwd is not None
    q_c, k_c, v_c, g_c, a_c = cols

    def rows(b, t):
        return b * nblk + _seq_block(t, nblk, reverse)

    def const(shape):
        return pl.BlockSpec(shape, lambda b, t: (0, 0))

    q_specs, k_specs, v_specs = (_col_specs(c0, w, tb, rows) for c0, w in ((q_c, kw), (k_c, kw), (v_c, vw)))
    g_specs = _col_specs(g_c, vw, tb, rows) if finish else []
    in_specs = q_specs + k_specs + v_specs + _col_specs(a_c, LANE, tb, rows)
    in_specs += [const((3 * LANE, kw)), const((1, kw)), const((LANE, LANE)), const((1, LANE))]
    args = [proj] * (len(q_specs) + len(k_specs) + len(v_specs) + 1) + [wa_st, ba, w_bound, bias_bound]
    if finish:
        in_specs += [pl.BlockSpec((tb, vw), lambda b, t: (rows(b, t), 0))] + g_specs + [const((1, vw))]
        args += [o_fwd] + [proj] * len(g_specs) + [gain]
    return pl.pallas_call(
        functools.partial(_glac_body, counts=(len(q_specs), len(k_specs), len(v_specs), len(g_specs)), dk=dk,
                          reverse=reverse, finish=finish, q_scale=dk ** -0.5),
        grid=(nb, nblk),
        in_specs=in_specs,
        out_specs=pl.BlockSpec((tb, vw), lambda b, t: (rows(b, t), 0)),
        out_shape=jax.ShapeDtypeStruct((m, vw), BF16 if finish else F32),
        scratch_shapes=[pltpu.VMEM((C_HEADS, vw // C_HEADS, kp), F32),
                        pltpu.VMEM((C_HEADS * (tb // GLA_CHUNK), GLA_CHUNK, GLA_CHUNK), F32)],
        compiler_params=_cp("parallel", "arbitrary"),
        name="gla_bwd" if reverse else "gla_fwd",
    )(*args)


def _lru_body(x_ref, cw_ref, cb_ref, w_ref, bias_ref, lam_ref, o_ref, xc_ref, *, nblk, n_ctx_blk):
    rb = TOKEN_BLOCK
    total = x_ref.shape[0]
    gw = x_ref.shape[1]
    rid = lax.broadcasted_iota(jnp.int32, (SUBLANE, gw), 0)
    cw = cw_ref[...]
    cb = cb_ref[...]

    def conv(blk, start):
        cur = x_ref[pl.ds(start, rb), :]
        seg_start = jnp.logical_or(blk == 0, blk == n_ctx_blk)
        seg_end = jnp.logical_or(blk == n_ctx_blk - 1, blk == nblk - 1)
        pstart = pl.multiple_of(jnp.maximum(start - SUBLANE, 0), SUBLANE)
        nstart = pl.multiple_of(jnp.minimum(start + rb, total - SUBLANE), SUBLANE)
        prev8 = jnp.where(seg_start, 0.0, x_ref[pl.ds(pstart, SUBLANE), :])
        next8 = jnp.where(seg_end, 0.0, x_ref[pl.ds(nstart, SUBLANE), :])
        ext = jnp.concatenate([prev8, cur, next8], axis=0)
        n_ext = rb + 2 * SUBLANE

        def tap(shift):
            return pltpu.roll(ext, shift % n_ext, axis=0)[SUBLANE:SUBLANE + rb]

        return cb + cw[0:1] * tap(2) + cw[1:2] * tap(1) + cw[2:3] * cur + cw[3:4] * tap(-1)

    def run(reverse):
        bw = B_BW
        wsl = slice(2 * bw, 4 * bw) if reverse else slice(0, 2 * bw)
        lam = lam_ref[1:2] if reverse else lam_ref[0:1]
        neg_c_softplus = -RG_C * (jnp.maximum(-lam, 0.0) + jnp.log(1.0 + jnp.exp(-jnp.abs(lam))))

        def stage(j):
            blk = _seq_block(j, nblk, reverse)
            start = pl.multiple_of(blk * rb, rb)
            if reverse:
                xc = xc_ref[pl.ds(start, rb), :]
            else:
                xc = conv(blk, start)
                xc_ref[pl.ds(start, rb), :] = xc
            parts = [jnp.dot(xc[:, t * bw:(t + 1) * bw].astype(BF16), w_ref[t][:, wsl],
                             preferred_element_type=F32) + bias_ref[t][:, wsl] for t in range(gw // bw)]
            r_pre = jnp.concatenate([p[:, :bw] for p in parts], axis=1)
            i_pre = jnp.concatenate([p[:, bw:] for p in parts], axis=1)
            return xc, jnp.concatenate([r_pre, i_pre], axis=1)

        def body(j, h):
            xc, gates = stage(j)
            blk = _seq_block(j, nblk, reverse)
            start = pl.multiple_of(blk * rb, rb)
            r = jax.nn.sigmoid(gates[:, :gw])
            i = jax.nn.sigmoid(gates[:, gw:])
            a = jnp.exp(neg_c_softplus * r)
            y = 1.0 - a * a
            u = (y * lax.rsqrt(jnp.maximum(y, TINY))) * (i * xc)
            ngroups = rb // SUBLANE
            order = range(ngroups - 1, -1, -1) if reverse else range(ngroups)
            last = slice(0, 1) if reverse else slice(SUBLANE - 1, SUBLANE)
            scanned = [None] * ngroups
            for gi in order:
                gs = slice(gi * SUBLANE, (gi + 1) * SUBLANE)
                ag, ug = a[gs], u[gs]
                for d in (1, 2, 4):
                    keep = (rid < SUBLANE - d) if reverse else (rid >= d)
                    shift = SUBLANE - d if reverse else d
                    a_s = jnp.where(keep, pltpu.roll(ag, shift, axis=0), 1.0)
                    u_s = jnp.where(keep, pltpu.roll(ug, shift, axis=0), 0.0)
                    ug = ag * u_s + ug
                    ag = ag * a_s
                scanned[gi] = (ag, ug)
            carry = [None] * ngroups
            for gi in order:
                ag, ug = scanned[gi]
                carry[gi] = h
                h = ag[last] * h + ug[last]
            hblk = jnp.concatenate([scanned[gi][1] + scanned[gi][0] * carry[gi] for gi in range(ngroups)], axis=0)
            if reverse:
                o_ref[pl.ds(start, rb), :] = o_ref[pl.ds(start, rb), :] + hblk
            else:
                o_ref[pl.ds(start, rb), :] = hblk
            return h

        lax.fori_loop(0, nblk, body, jnp.zeros((1, gw), F32))

    run(False)
    run(True)


def _lru(proj, x_off, conv_w, conv_b, w_gates, b_gates, lam, nb, nblk, n_ctx_blk):
    m = proj.shape[0]
    nblocks = w_gates.shape[0]
    total = m // nb
    tiles = LRU_TILES_PER_STEP
    gw = tiles * B_BW
    assert x_off % tiles == 0 and nblocks % tiles == 0
    return pl.pallas_call(
        functools.partial(_lru_body, nblk=nblk, n_ctx_blk=n_ctx_blk),
        grid=(nb, nblocks // tiles),
        in_specs=[
            pl.BlockSpec((total, gw), lambda b, n: (b, x_off // tiles + n)),
            pl.BlockSpec((B_CONV, gw), lambda b, n: (0, n)),
            pl.BlockSpec((1, gw), lambda b, n: (0, n)),
            pl.BlockSpec((tiles, B_BW, 4 * B_BW), lambda b, n: (n, 0, 0)),
            pl.BlockSpec((tiles, 1, 4 * B_BW), lambda b, n: (n, 0, 0)),
            pl.BlockSpec((2, gw), lambda b, n: (0, n)),
        ],
        out_specs=pl.BlockSpec((total, gw), lambda b, n: (b, n)),
        out_shape=jax.ShapeDtypeStruct((m, nblocks * B_BW), F32),
        scratch_shapes=[pltpu.VMEM((total, gw), F32)],
        compiler_params=_cp("parallel", "parallel"),
        name="rglru_scan",
    )(proj, conv_w, conv_b, w_gates, b_gates, lam)


def _gelu_tanh(v):
    return 0.5 * v * (1.0 + jnp.tanh(0.7978845608028654 * (v + 0.044715 * (v * v * v))))


def _lru_fin_body(h_ref, *refs):
    gate_refs, (gain_ref, y_ref) = refs[:-2], refs[-2:]
    h = h_ref[...]
    y = h * lax.rsqrt(jnp.mean(h * h, axis=-1, keepdims=True) + EPS) * gain_ref[...]
    y_ref[...] = (y * _gelu_tanh(_cat(gate_refs, slice(None)))).astype(y_ref.dtype)


def _lru_fin(h, proj, gate_col, gain):
    m, w = h.shape
    tr = TOKEN_BLOCK
    gate_specs = _col_specs(gate_col, w, tr, lambda i: i)
    return pl.pallas_call(
        _lru_fin_body,
        grid=(m // tr,),
        in_specs=[pl.BlockSpec((tr, w), lambda i: (i, 0))] + gate_specs + [pl.BlockSpec((1, w), lambda i: (0, 0))],
        out_specs=pl.BlockSpec((tr, w), lambda i: (i, 0)),
        out_shape=jax.ShapeDtypeStruct((m, w), BF16),
        compiler_params=_cp("parallel"),
        name="rglru_finish",
    )(h, *([proj] * len(gate_specs)), gain)


def _conv_rows(up, mid, down, w, bias, u, width):
    n, tc = mid.shape
    left = w[3:4] * mid
    cent = w[4:5] * mid
    right = w[5:6] * mid
    if up is not None:
        left += w[0:1] * up
        cent += w[1:2] * up
        right += w[2:3] * up
    if down is not None:
        left += w[6:7] * down
        cent += w[7:8] * down
        right += w[8:9] * down
    cidx = lax.broadcasted_iota(jnp.int32, (n, tc), 0) & (width - 1)
    acc = cent + bias
    acc += jnp.where(cidx != 0, pltpu.roll(left, 1, axis=0), 0.0)
    acc += jnp.where(cidx != width - 1, pltpu.roll(right, n - 1, axis=0), 0.0)
    return (_silu(acc) * u).astype(BF16)


def _ffn_mid_body(u_ref, g_ref, w_ref, b_ref, o_ref, *, ctx_rows):
    total = u_ref.shape[0]
    w = w_ref[...]
    bias = b_ref[...]
    o_ref[0:ctx_rows, :] = _conv_rows(None, g_ref[0:ctx_rows, :], None, w, bias, u_ref[0:ctx_rows, :], ctx_rows)
    step = FFN_ROWS
    nsteps = (total - ctx_rows) // step
    zeros = jnp.zeros((GRID_W, u_ref.shape[1]), F32)

    def load(start, n):
        return g_ref[pl.ds(start, n), :]

    def emit(start, up, mid, down):
        o_ref[pl.ds(start, step), :] = _conv_rows(up, mid, down, w, bias, u_ref[pl.ds(start, step), :], GRID_W)

    first = ctx_rows
    last = ctx_rows + (nsteps - 1) * step
    if nsteps == 1:
        mid = load(first, step)
        emit(first, jnp.concatenate([zeros, mid[:step - GRID_W]], axis=0), mid,
             jnp.concatenate([mid[GRID_W:], zeros], axis=0))
        return
    mid = load(first, step)
    emit(first, jnp.concatenate([zeros, mid[:step - GRID_W]], axis=0), mid, load(first + GRID_W, step))

    def body(s, carry):
        start = pl.multiple_of(ctx_rows + s * step, GRID_W)
        emit(start, load(start - GRID_W, step), load(start, step), load(start + GRID_W, step))
        return carry

    lax.fori_loop(1, nsteps - 1, body, 0)
    mid = load(last, step)
    emit(last, load(last - GRID_W, step), mid, jnp.concatenate([mid[GRID_W:], zeros], axis=0))


def _ffn_mid(up, conv_w9, conv_b, nb, ctx_rows, tc=LANE):
    m, two_f = up.shape
    f = two_f // 2
    total = m // nb
    nct = f // tc
    return pl.pallas_call(
        functools.partial(_ffn_mid_body, ctx_rows=ctx_rows),
        grid=(nb, nct),
        in_specs=[
            pl.BlockSpec((total, tc), lambda b, j: (b, j)),
            pl.BlockSpec((total, tc), lambda b, j: (b, nct + j)),
            pl.BlockSpec((9, tc), lambda b, j: (0, j)),
            pl.BlockSpec((1, tc), lambda b, j: (0, j)),
        ],
        out_specs=pl.BlockSpec((total, tc), lambda b, j: (b, j)),
        out_shape=jax.ShapeDtypeStruct((m, f), BF16),
        compiler_params=_cp("parallel", "parallel"),
        name="ffn_conv_gate",
    )(up, up, conv_w9, conv_b)


def _final_norm_body(x_ref, g_ref, o_ref):
    x = x_ref[...]
    o_ref[...] = x * lax.rsqrt(jnp.mean(x * x, axis=-1, keepdims=True) + EPS) * g_ref[...]


def _final_norm(x, g, nb, nblk, n_ctx_blk, seq):
    d = x.shape[1]
    tr = TOKEN_BLOCK
    return pl.pallas_call(
        _final_norm_body,
        grid=(nb, seq // tr),
        in_specs=[
            pl.BlockSpec((tr, d), lambda b, i: (b * nblk + n_ctx_blk + i, 0)),
            pl.BlockSpec((1, d), lambda b, i: (0, 0)),
        ],
        out_specs=pl.BlockSpec((None, tr, d), lambda b, i: (b, i, 0)),
        out_shape=jax.ShapeDtypeStruct((nb, seq, d), F32),
        compiler_params=_cp("parallel", "parallel"),
        name="final_norm",
    )(x, g.reshape(1, d))


def _lower_bounds_body(x_ref, o_ref):
    depth = x_ref.shape[0]
    rows = [x_ref[i:i + 1, :] for i in range(depth)]
    top = functools.reduce(jnp.maximum, rows)
    es = [jnp.exp(r - top) for r in rows]
    total = functools.reduce(lambda a, b: a + b, es)
    ps = [e / total for e in es]
    run = None
    for i, p in enumerate(ps):
        run = p if run is None else run + p
        o_ref[i:i + 1, :] = jnp.clip(run - ps[0], 0.0, 1.0)


def _lower_bounds(logits):
    return pl.pallas_call(
        _lower_bounds_body,
        out_shape=jax.ShapeDtypeStruct(logits.shape, F32),
        name="hgrn_lower_bounds",
    )(logits.astype(F32))


def _pick_tile(total, candidates):
    for t in candidates:
        if total % t == 0:
            return t
    raise ValueError(f"no tile for {total}")


def kernel(x, c, ctx, c_ctx, w_ada, b_ada, norm1_g, w_in, hgrn_lb_fwd, hgrn_lb_bwd, hgrn_norm_g, lru_conv_w, lru_conv_b, lru_wr_fwd, lru_br_fwd, lru_wi_fwd, lru_bi_fwd, lru_lam_fwd, lru_wr_bwd, lru_br_bwd, lru_wi_bwd, lru_bi_bwd, lru_lam_bwd, lru_norm_g, gla_wa_fwd, gla_ba_fwd, gla_wa_bwd, gla_ba_bwd, gla_norm_g, w_out, norm2_g, w_up, ffn_conv_w, ffn_conv_b, w_down, final_norm_g):
    nb, seq, d = x.shape
    ctx_len = ctx.shape[1]
    depth = w_ada.shape[0]
    assert ctx_len == TOKEN_BLOCK and seq % FFN_ROWS == 0
    rows_per_batch = ctx_len + seq
    nblk = rows_per_batch // TOKEN_BLOCK
    n_ctx_blk = ctx_len // TOKEN_BLOCK
    m = nb * rows_per_batch

    a_w = hgrn_norm_g.shape[1]
    b_w = lru_norm_g.shape[1]
    c_w = gla_norm_g.shape[1]
    c_k = gla_ba_fwd.shape[1]
    c_dk = c_k // C_HEADS
    n_blocks = b_w // B_BW

    col_bx = 5 * a_w
    col_bg = col_bx + b_w
    col_cq = col_bg + b_w
    col_ck = col_cq + c_k
    col_cv = col_ck + c_k
    col_cg = col_cv + c_w
    col_ab = col_cg + c_w
    assert col_ab % LANE == 0 and col_ab + 2 * C_RANK == w_in.shape[2]
    tn_in = 512
    n_proj = -(-w_in.shape[2] // tn_in) * tn_in

    tm = _pick_tile(rows_per_batch, (768, 512, 256))
    tm_in = _pick_tile(rows_per_batch, (1408, 768, 512, 256))

    xs = jnp.concatenate([ctx, x], axis=1).reshape(m, d)

    c16 = jnp.zeros((16, d), F32).at[:nb].set(c).at[nb].set(c_ctx)
    mods = _ada(c16, w_ada, b_ada)
    lbf_all = _lower_bounds(hgrn_lb_fwd)
    lbb_all = _lower_bounds(hgrn_lb_bwd)

    w_in_b = _cast_pad(jnp.swapaxes(w_in, 1, 2), n_proj)
    w_up_b = w_up.astype(BF16)
    w_down_b = w_down.astype(BF16)
    wo_parts = [w_out[:, :a_w].astype(BF16), w_out[:, a_w:a_w + b_w].astype(BF16), w_out[:, a_w + b_w:].astype(BF16)]

    def gate_params(wa, ba, row0):
        wp = jnp.zeros((LANE, c_k), F32).at[row0:row0 + C_RANK].set(wa)
        hi = wp.astype(BF16)
        lo = (wp - hi.astype(F32)).astype(BF16)
        w_bound = jnp.broadcast_to(jnp.max(jnp.abs(wp), axis=1, keepdims=True) * BOUND_MARGIN, (LANE, LANE))
        bias_bound = jnp.full((1, LANE), LOG2, F32) + jnp.max(jnp.abs(ba))
        return jnp.concatenate([hi, hi, lo], axis=0), ba.reshape(1, c_k), w_bound.astype(BF16), bias_bound

    for l in range(depth):
        mod = mods[l, :nb + 1].reshape(nb + 1, 6, 1, d)
        sh1, sc1, g1, sh2, sc2, g2 = (mod[:, i] for i in range(6))

        h = _norm_mod(xs, norm1_g[l], sh1, sc1, nblk, n_ctx_blk)
        proj = _matmul(h, w_in_b, l, tm_in, tn_in, a_resident=True, b_transposed=True)

        lbf = lbf_all[l].reshape(1, a_w)
        lbb = lbb_all[l].reshape(1, a_w)
        oa = _hgrn(proj, lbf, nb, nblk, reverse=False)
        ya = _hgrn(proj, lbb, nb, nblk, reverse=True, o_fwd=oa, gain=hgrn_norm_g[l].reshape(1, a_w))

        w_gates = jnp.concatenate([lru_wr_fwd[l], lru_wi_fwd[l], lru_wr_bwd[l], lru_wi_bwd[l]], axis=-1).astype(BF16)
        b_gates = jnp.concatenate([t[l].reshape(n_blocks, 1, B_BW) for t in
                                   (lru_br_fwd, lru_bi_fwd, lru_br_bwd, lru_bi_bwd)], axis=-1)
        lam = jnp.stack([lru_lam_fwd[l], lru_lam_bwd[l]], axis=0)
        hb = _lru(proj, col_bx // LANE, lru_conv_w[l], lru_conv_b[l].reshape(1, b_w), w_gates, b_gates, lam,
                  nb, nblk, n_ctx_blk)
        yb = _lru_fin(hb, proj, col_bg, lru_norm_g[l].reshape(1, b_w))

        cols = (col_cq, col_ck, col_cv, col_cg, col_ab)
        gain_c = gla_norm_g[l].reshape(1, c_w)
        oc = _glac(proj, cols, gate_params(gla_wa_fwd[l], gla_ba_fwd[l], 0), c_dk, c_w, nb, nblk, reverse=False)
        yc = _glac(proj, cols, gate_params(gla_wa_bwd[l], gla_ba_bwd[l], C_RANK), c_dk, c_w, nb, nblk,
                   reverse=True, o_fwd=oc, gain=gain_c)

        xs = _matmul_res([ya, yb, yc], wo_parts, l, xs, g1, tm, 1024, rows_per_batch, ctx_len)

        h2 = _norm_mod(xs, norm2_g[l], sh2, sc2, nblk, n_ctx_blk)
        up = _matmul(h2, w_up_b, l, tm, 1024)
        act = _ffn_mid(up, ffn_conv_w[l].reshape(9, -1), ffn_conv_b[l].reshape(1, -1), nb, ctx_len)
        xs = _matmul_res([act], [w_down_b], l, xs, g2, tm, 1024, rows_per_batch, ctx_len)

    return _final_norm(xs, final_norm_g, nb, nblk, n_ctx_blk, seq)
```
